```python
import math
import jax, jax.numpy as jnp
from jax import lax
import numpy as np

D_MODEL = 1024
BATCH = 16
SEQ = 4096
DEPTH = 2

CHUNK = 64
HEAD_DIM = 64
H_A = 4
KV_RANK = 128
IDX_HEADS = 4
IDX_DIM = 64
TOPK_MAX = 256
QBLK = 128
H_B = 4
DK_B = 64
DV_B = 128
GATE_RANK = 16
GATE_TEMP = 16.0
H_C = 4
KV_C = 2
WINDOW = 128
WIN_CHUNKS = WINDOW // CHUNK
NUM_BUCKETS = 32
MAX_DISTANCE = 128
D_FF = ((8 * D_MODEL // 3 + 255) // 256) * 256
D_MIX = H_A * HEAD_DIM + H_B * DV_B + H_C * HEAD_DIM
IN_SPLITS = (H_A * HEAD_DIM, KV_RANK, IDX_HEADS * IDX_DIM, IDX_DIM, IDX_HEADS,
             H_B * DK_B, H_B * DK_B, H_B * DV_B, GATE_RANK, H_B * DV_B,
             H_C * HEAD_DIM, KV_C * HEAD_DIM, KV_C * HEAD_DIM)
D_IN = sum(IN_SPLITS)
EPS = 1e-6

kernel_name = 'hybrid_dsa_gla_swa_encoder'


def rms_norm(x, g):
    x32 = x.astype(jnp.float32)
    y = x32 * lax.rsqrt(jnp.mean(x32 * x32, axis=-1, keepdims=True) + EPS)
    return (y * g.astype(jnp.float32)).astype(x.dtype)


def rel_bucket(rel):
    nb = NUM_BUCKETS // 2
    max_exact = nb // 2
    n = jnp.abs(rel)
    side = jnp.where(rel > 0, nb, 0)
    nf = jnp.maximum(n, 1).astype(jnp.float32)
    large = max_exact + (jnp.log(nf / max_exact) / math.log(MAX_DISTANCE / max_exact)
                         * (nb - max_exact)).astype(jnp.int32)
    large = jnp.minimum(large, nb - 1)
    return side + jnp.where(n < max_exact, n, large)


def split_columns(p):
    offs, acc = [], 0
    for s in IN_SPLITS[:-1]:
        acc += s
        offs.append(acc)
    return jnp.split(p, offs, axis=-1)


def dsa_mixer(q_a, c_kv, q_idx, k_idx, w_idx, kv_norm, w_uk, w_uv, bias_table):
    bsz, s_len, _ = q_a.shape
    topk = min(TOPK_MAX, s_len // 4)
    nblk = s_len // QBLK
    c_kv = rms_norm(c_kv, kv_norm)
    q = q_a.reshape(bsz, s_len, H_A, HEAD_DIM)
    q_lat = jnp.einsum('bshd,rhd->bshr', q, w_uk) * HEAD_DIM ** -0.5
    q_idx = q_idx.reshape(bsz, s_len, IDX_HEADS, IDX_DIM)
    w_idx = w_idx * (IDX_HEADS ** -0.5 * IDX_DIM ** -0.5)
    key_chunk = jnp.arange(s_len, dtype=jnp.int32) // CHUNK
    table = bias_table[:, :H_A]

    def to_blocks(a):
        return jnp.moveaxis(a.reshape(bsz, nblk, QBLK, *a.shape[2:]), 1, 0)

    def block(args):
        ql, qi, wi, pos = args
        s_idx = jax.nn.relu(jnp.einsum('bqhd,bsd->bqhs', qi, k_idx))
        score = jnp.einsum('bqhs,bqh->bqs', s_idx, wi).astype(jnp.float32)
        q_chunk = pos // CHUNK
        admissible = key_chunk[None, :] <= q_chunk[:, None]
        score = jnp.where(admissible[None], score, -jnp.inf)
        _, sel = lax.top_k(score, topk)
        valid = (sel // CHUNK) <= q_chunk[None, :, None]
        c_sel = jax.vmap(lambda c, i: c[i])(c_kv, sel)
        logits = jnp.einsum('bqhr,bqkr->bhqk', ql, c_sel).astype(jnp.float32)
        bias = table[rel_bucket(sel - pos[None, :, None])]
        logits = logits + jnp.moveaxis(bias, -1, 1).astype(jnp.float32)
        logits = jnp.where(valid[:, None], logits, -jnp.inf)
        p = jax.nn.softmax(logits, axis=-1).astype(c_sel.dtype)
        return jnp.einsum('bhqk,bqkr->bqhr', p, c_sel)

    pos_blocks = jnp.arange(s_len, dtype=jnp.int32).reshape(nblk, QBLK)
    o_lat = lax.map(block, (to_blocks(q_lat), to_blocks(q_idx), to_blocks(w_idx), pos_blocks))
    o_lat = jnp.moveaxis(o_lat, 0, 1).reshape(bsz, s_len, H_A, KV_RANK)
    o = jnp.einsum('bshr,rhd->bshd', o_lat, w_uv)
    return o.reshape(bsz, s_len, H_A * HEAD_DIM)


def gla_mixer(q_b, k_b, v_b, g_lr, r_b, w_g2, b_g, gn):
    bsz, s_len, _ = q_b.shape
    n = s_len // CHUNK
    f32 = jnp.float32
    g = jax.nn.log_sigmoid((g_lr @ w_g2 + b_g).astype(f32)) / GATE_TEMP

    def heads(a, d):
        return a.reshape(bsz, n, CHUNK, H_B, d).transpose(0, 3, 1, 2, 4).astype(f32)

    q = heads(q_b, DK_B) * DK_B ** -0.5
    k = heads(k_b, DK_B)
    v = heads(v_b, DV_B)
    g = heads(g, DK_B)
    b = jnp.cumsum(g, axis=3)
    b_last = b[:, :, :, -1:, :]
    qe = q * jnp.exp(b)
    ke = k * jnp.exp(-b)
    kd = k * jnp.exp(b_last - b)
    causal = jnp.tril(jnp.ones((CHUNK, CHUNK), dtype=bool))
    att = jnp.where(causal, jnp.einsum('bhnid,bhnjd->bhnij', qe, ke), 0.0)
    o_intra = jnp.einsum('bhnij,bhnje->bhnie', att, v)
    kv = jnp.einsum('bhncd,bhnce->bhnde', kd, v)
    decay = jnp.exp(b_last[:, :, :, 0, :])

    def step(state, inp):
        d, kvn = inp
        return d[..., None] * state + kvn, state

    _, s_prev = lax.scan(step, jnp.zeros((bsz, H_B, DK_B, DV_B), f32),
                         (jnp.moveaxis(decay, 2, 0), jnp.moveaxis(kv, 2, 0)))
    s_prev = jnp.moveaxis(s_prev, 0, 2)
    o = o_intra + jnp.einsum('bhncd,bhnde->bhnce', qe, s_prev)
    o = o.transpose(0, 2, 3, 1, 4).reshape(bsz, s_len, H_B, DV_B)
    o = o * lax.rsqrt(jnp.mean(o * o, axis=-1, keepdims=True) + EPS) * gn.reshape(H_B, DV_B).astype(f32)
    o = o.reshape(bsz, s_len, H_B * DV_B) * jax.nn.silu(r_b.astype(f32))
    return o.astype(q_b.dtype)


def swa_mixer(q_c, k_c, v_c, sinks, bias_table):
    bsz, s_len, _ = q_c.shape
    n = s_len // CHUNK
    grp = H_C // KV_C
    band = (WIN_CHUNKS + 1) * CHUNK
    q = q_c.reshape(bsz, n, CHUNK, KV_C, grp, HEAD_DIM)

    def banded(a):
        a = a.reshape(bsz, n, CHUNK, KV_C, HEAD_DIM)
        a = jnp.pad(a, ((0, 0), (WIN_CHUNKS, 0), (0, 0), (0, 0), (0, 0)))
        return jnp.concatenate([a[:, w:w + n] for w in range(WIN_CHUNKS + 1)], axis=2)

    k = banded(k_c)
    v = banded(v_c)
    logits = jnp.einsum('bnqkgd,bnskd->bnkgqs', q, k).astype(jnp.float32) * HEAD_DIM ** -0.5
    qi = jnp.arange(CHUNK, dtype=jnp.int32)[:, None]
    kj = jnp.arange(band, dtype=jnp.int32)[None, :]
    rel = kj - WIN_CHUNKS * CHUNK - qi
    bias = bias_table[:, H_A:][rel_bucket(rel)]
    bias = bias.transpose(2, 0, 1).reshape(KV_C, grp, CHUNK, band).astype(jnp.float32)
    valid = (jnp.arange(n, dtype=jnp.int32)[:, None] * CHUNK + kj - WIN_CHUNKS * CHUNK) >= 0
    logits = jnp.where(valid[None, :, None, None, None, :], logits + bias, -jnp.inf)
    sink = sinks.reshape(KV_C, grp)[:, :, None, None].astype(jnp.float32)
    m = jnp.maximum(jnp.max(logits, axis=-1, keepdims=True), sink)
    e = jnp.exp(logits - m)
    p = (e / (jnp.sum(e, axis=-1, keepdims=True) + jnp.exp(sink - m))).astype(v.dtype)
    o = jnp.einsum('bnkgqs,bnskd->bnqkgd', p, v)
    return o.reshape(bsz, s_len, H_C * HEAD_DIM)


def setup_inputs(seed: int = 0) -> dict:
    key = jax.random.key(seed)
    ks = jax.random.split(key, 20)
    nrm = jax.random.normal
    f32 = jnp.float32
    return {
        'x': nrm(ks[0], (BATCH, SEQ, D_MODEL), f32),
        'w_in': nrm(ks[1], (DEPTH, D_MODEL, D_IN), f32) * D_MODEL ** -0.5,
        'w_out': nrm(ks[2], (DEPTH, D_MIX, D_MODEL), f32) * D_MIX ** -0.5,
        'norm_mix': 1.0 + 0.01 * nrm(ks[3], (DEPTH, D_MODEL), f32),
        'norm_ffn': 1.0 + 0.01 * nrm(ks[4], (DEPTH, D_MODEL), f32),
        'kv_norm': 1.0 + 0.01 * nrm(ks[5], (DEPTH, KV_RANK), f32),
        'w_uk': nrm(ks[6], (DEPTH, KV_RANK, H_A, HEAD_DIM), f32) * KV_RANK ** -0.5,
        'w_uv': nrm(ks[7], (DEPTH, KV_RANK, H_A, HEAD_DIM), f32) * KV_RANK ** -0.5,
        'w_gate2': nrm(ks[8], (DEPTH, GATE_RANK, H_B * DK_B), f32) * GATE_RANK ** -0.5,
        'b_gate': 0.1 * nrm(ks[9], (DEPTH, H_B * DK_B), f32),
        'gla_norm': 1.0 + 0.01 * nrm(ks[10], (DEPTH, H_B * DV_B), f32),
        'sinks': nrm(ks[11], (DEPTH, H_C), f32),
        'rel_bias': 0.5 * nrm(ks[12], (NUM_BUCKETS, H_A + H_C), f32),
        'w_ffn_gate': nrm(ks[13], (DEPTH, D_MODEL, D_FF), f32) * D_MODEL ** -0.5,
        'w_ffn_up': nrm(ks[14], (DEPTH, D_MODEL, D_FF), f32) * D_MODEL ** -0.5,
        'w_ffn_down': nrm(ks[15], (DEPTH, D_FF, D_MODEL), f32) * D_FF ** -0.5,
        'final_norm': 1.0 + 0.01 * nrm(ks[16], (D_MODEL,), f32),
    }


def reference(x, w_in, w_out, norm_mix, norm_ffn, kv_norm, w_uk, w_uv, w_gate2, b_gate,
              gla_norm, sinks, rel_bias, w_ffn_gate, w_ffn_up, w_ffn_down, final_norm):
    for l in range(DEPTH):
        h = rms_norm(x, norm_mix[l])
        p = h @ w_in[l]
        (q_a, c_kv, q_idx, k_idx, w_idx, q_b, k_b, v_b, g_lr, r_b,
         q_c, k_c, v_c) = split_columns(p)
        out_a = dsa_mixer(q_a, c_kv, q_idx, k_idx, w_idx, kv_norm[l], w_uk[l], w_uv[l], rel_bias)
        out_b = gla_mixer(q_b, k_b, v_b, g_lr, r_b, w_gate2[l], b_gate[l], gla_norm[l])
        out_c = swa_mixer(q_c, k_c, v_c, sinks[l], rel_bias)
        x = x + jnp.concatenate([out_a, out_b, out_c], axis=-1) @ w_out[l]
        h = rms_norm(x, norm_ffn[l])
        x = x + (jax.nn.silu(h @ w_ffn_gate[l]) * (h @ w_ffn_up[l])) @ w_ffn_down[l]
    return rms_norm(x, final_norm)
```

```python
import functools
import math

import jax
import jax.numpy as jnp
from jax import lax
from jax.experimental import pallas as pl
from jax.experimental.pallas import tpu as pltpu

D_MODEL = 1024
DEPTH = 2
CHUNK = 64
HEAD_DIM = 64
H_A = 4
KV_RANK = 128
IDX_HEADS = 4
IDX_DIM = 64
TOPK_MAX = 256
H_B = 4
DK_B = 64
DV_B = 128
GATE_RANK = 16
GATE_TEMP = 16.0
H_C = 4
KV_C = 2
WINDOW = 128
WIN_CHUNKS = WINDOW // CHUNK
NUM_BUCKETS = 32
MAX_DISTANCE = 128
D_FF = ((8 * D_MODEL // 3 + 255) // 256) * 256
EPS = 1e-6

IN_SPLITS = (H_A * HEAD_DIM, KV_RANK, IDX_HEADS * IDX_DIM, IDX_DIM, IDX_HEADS,
             H_B * DK_B, H_B * DK_B, H_B * DV_B, GATE_RANK, H_B * DV_B,
             H_C * HEAD_DIM, KV_C * HEAD_DIM, KV_C * HEAD_DIM)

LANES = 128
SUBLANES = 8
VMEM_LIMIT = 56 * 1024 * 1024

MXU_DTYPE = jnp.bfloat16
F32 = jnp.float32
NEG_INF = float("-inf")
F32_MAX = float(jnp.finfo(jnp.float32).max)
INT_MIN = -2 ** 31

QBLK = 128
TM_PROJ = 512
TS_GLA = 256
TQ_SWA = 256
TM_FFN = 512
TF_FFN = D_FF // 2


def _dot(a, b):
    return jnp.dot(a, b, preferred_element_type=F32)


def _dot_nt(a, b):
    return lax.dot_general(a, b, (((1,), (1,)), ((), ())), preferred_element_type=F32)


def _dot_tn(a, b):
    return lax.dot_general(a, b, (((0,), (0,)), ((), ())), preferred_element_type=F32)


def _rms(x, g):
    return x * lax.rsqrt(jnp.mean(x * x, axis=-1, keepdims=True) + EPS) * g


_C_QA = 0
_C_CKV = 256
_C_QIDX = 384
_C_KW = 640
_C_QB = 768
_C_KB = 1024
_C_VB = 1280
_C_GLR = 1792
_C_RB = 1920
_C_QC = 2432
_C_KC = 2688
_C_VC = 2816
_NP = 2944


def _proj_kernel(x_ref, g_ref, w_ref, wuk_ref, kvn_ref,
                 qlat_ref, ckv_ref, ckvt_ref, qidx_ref, kidx_ref, widxt_ref,
                 qb_ref, kb_ref, vb_ref, glr_ref, rb_ref, qc_ref, kc_ref, vc_ref):
    h = _rms(x_ref[...], g_ref[...]).astype(MXU_DTYPE)

    def seg(c0, width):
        return _dot(h, w_ref[:, c0:c0 + width])

    qa = seg(_C_QA, 256)
    for hd in range(H_A):
        qh = qa[:, hd * HEAD_DIM:(hd + 1) * HEAD_DIM].astype(MXU_DTYPE)
        ql = _dot(qh, wuk_ref[hd]) * (HEAD_DIM ** -0.5)
        qlat_ref[:, hd * KV_RANK:(hd + 1) * KV_RANK] = ql.astype(qlat_ref.dtype)

    ckv = _rms(seg(_C_CKV, KV_RANK), kvn_ref[...])
    ckv_ref[...] = ckv.astype(ckv_ref.dtype)
    for t in range(TM_PROJ // QBLK):
        ckvt_ref[0, t] = ckv[t * QBLK:(t + 1) * QBLK, :].T.astype(ckvt_ref.dtype)

    qidx_ref[...] = seg(_C_QIDX, 256).astype(qidx_ref.dtype)
    kw = seg(_C_KW, LANES)
    kidx_ref[...] = kw[:, :IDX_DIM].astype(kidx_ref.dtype)
    kwt = kw.T
    widxt_ref[0] = kwt[IDX_DIM:IDX_DIM + SUBLANES, :] * (IDX_HEADS ** -0.5 * IDX_DIM ** -0.5)

    qb_ref[...] = seg(_C_QB, 256).astype(qb_ref.dtype)
    kb_ref[...] = seg(_C_KB, 256).astype(kb_ref.dtype)
    vb_ref[...] = seg(_C_VB, 512).astype(vb_ref.dtype)
    glr_ref[...] = seg(_C_GLR, LANES)
    rb_ref[...] = seg(_C_RB, 512).astype(rb_ref.dtype)
    qc_ref[...] = seg(_C_QC, 256).astype(qc_ref.dtype)
    kc_ref[...] = seg(_C_KC, LANES).astype(kc_ref.dtype)
    vc_ref[...] = seg(_C_VC, LANES).astype(vc_ref.dtype)


def _arrange_w_in(w):
    offs = [0]
    for s in IN_SPLITS:
        offs.append(offs[-1] + s)
    parts = [w[:, offs[i]:offs[i + 1]] for i in range(len(IN_SPLITS))]
    (qa, ckv, qidx, kidx, widx, qb, kb, vb, glr, rb, qc, kc, vc) = parts
    z = lambda n: jnp.zeros((w.shape[0], n), w.dtype)
    return jnp.concatenate(
        [qa, ckv, qidx, kidx, widx, z(LANES - IDX_DIM - IDX_HEADS), qb, kb, vb,
         glr, z(LANES - GATE_RANK), rb, qc, kc, vc], axis=1)


def _proj_call(x2, g, w, wuk, kvn, bsz, s_len):
    t_tok = x2.shape[0]
    tm = TM_PROJ
    n_s = s_len // tm
    act = MXU_DTYPE

    def tok(width):
        return pl.BlockSpec((tm, width), lambda i: (i, 0))

    def const(shape):
        return pl.BlockSpec(shape, lambda i: (0,) * len(shape))

    out_shape = [
        jax.ShapeDtypeStruct((t_tok, H_A * KV_RANK), act),
        jax.ShapeDtypeStruct((t_tok, KV_RANK), act),
        jax.ShapeDtypeStruct((bsz, s_len // QBLK, KV_RANK, QBLK), act),
        jax.ShapeDtypeStruct((t_tok, IDX_HEADS * IDX_DIM), act),
        jax.ShapeDtypeStruct((t_tok, IDX_DIM), act),
        jax.ShapeDtypeStruct((bsz, SUBLANES, s_len), F32),
        jax.ShapeDtypeStruct((t_tok, H_B * DK_B), act),
        jax.ShapeDtypeStruct((t_tok, H_B * DK_B), act),
        jax.ShapeDtypeStruct((t_tok, H_B * DV_B), act),
        jax.ShapeDtypeStruct((t_tok, LANES), F32),
        jax.ShapeDtypeStruct((t_tok, H_B * DV_B), act),
        jax.ShapeDtypeStruct((t_tok, H_C * HEAD_DIM), act),
        jax.ShapeDtypeStruct((t_tok, KV_C * HEAD_DIM), act),
        jax.ShapeDtypeStruct((t_tok, KV_C * HEAD_DIM), act),
    ]
    out_specs = [
        tok(H_A * KV_RANK), tok(KV_RANK),
        pl.BlockSpec((1, tm // QBLK, KV_RANK, QBLK), lambda i: (i // n_s, i % n_s, 0, 0)),
        tok(IDX_HEADS * IDX_DIM), tok(IDX_DIM),
        pl.BlockSpec((1, SUBLANES, tm), lambda i: (i // n_s, 0, i % n_s)),
        tok(H_B * DK_B), tok(H_B * DK_B), tok(H_B * DV_B), tok(LANES), tok(H_B * DV_B),
        tok(H_C * HEAD_DIM), tok(KV_C * HEAD_DIM), tok(KV_C * HEAD_DIM),
    ]
    return pl.pallas_call(
        _proj_kernel,
        grid=(t_tok // tm,),
        in_specs=[tok(D_MODEL), const((1, D_MODEL)), const((D_MODEL, _NP)),
                  const((H_A, HEAD_DIM, KV_RANK)), const((1, KV_RANK))],
        out_specs=out_specs,
        out_shape=out_shape,
        compiler_params=pltpu.CompilerParams(
            dimension_semantics=("parallel",), vmem_limit_bytes=VMEM_LIMIT),
        name="norm_in_proj",
    )(x2, g, w, wuk, kvn)


def _key_to_float(k):
    bits = jnp.where(k >= 0, k, INT_MIN - k)
    return lax.bitcast_convert_type(bits, F32)


def _dsa_kernel(qidx_ref, widxt_ref, qlat_ref, kidx_ref, ckv_ref, ckvt_ref,
                biasn_ref, biasf_ref, wuv_ref, out_ref,
                s_ref, m_ref, l_ref, acc_ref, *, s_len, topk):
    jq = pl.program_id(1)
    q0 = jq * QBLK
    n_tiles = jq + 1
    n_virtual = s_len - n_tiles * QBLK

    wt = widxt_ref[0]
    qchunk = (q0 + lax.broadcasted_iota(jnp.int32, (1, QBLK), 1)) // CHUNK
    krow = lax.broadcasted_iota(jnp.int32, (QBLK, 1), 0)

    def score_tile(j, carry):
        k0 = pl.multiple_of(j * QBLK, QBLK)
        kt = kidx_ref[0, pl.ds(k0, QBLK), :]
        s = jnp.zeros((QBLK, QBLK), F32)
        for hd in range(IDX_HEADS):
            qi = qidx_ref[0, :, hd * IDX_DIM:(hd + 1) * IDX_DIM]
            s = s + jnp.maximum(_dot_nt(kt, qi), 0.0) * wt[hd:hd + 1, :]
        kchunk = (k0 + krow) // CHUNK
        s_ref[pl.ds(k0, QBLK), :] = jnp.where(kchunk <= qchunk, s, NEG_INF)
        return carry

    lax.fori_loop(0, n_tiles, score_tile, 0)

    def count_ge(tf):
        def body(j, c8):
            k0 = pl.multiple_of(j * QBLK, QBLK)
            hit = (s_ref[pl.ds(k0, QBLK), :] >= tf).astype(jnp.int32)
            return c8 + jnp.sum(hit.reshape(QBLK // SUBLANES, SUBLANES, QBLK), axis=0)

        c8 = lax.fori_loop(0, n_tiles, body, jnp.zeros((SUBLANES, QBLK), jnp.int32))
        virt = jnp.where(NEG_INF >= tf, n_virtual, 0)
        return jnp.sum(c8, axis=0, keepdims=True) + virt

    def bit_step(i, carry):
        t_key, cnt_t = carry
        cand = t_key + jnp.left_shift(jnp.int32(1), 31 - i)
        c = count_ge(_key_to_float(cand))
        ok = c >= topk
        return jnp.where(ok, cand, t_key), jnp.where(ok, c, cnt_t)

    t_key, cnt_t = lax.fori_loop(
        0, 32, bit_step,
        (jnp.full((1, QBLK), INT_MIN, jnp.int32), jnp.full((1, QBLK), s_len, jnp.int32)))
    tf = _key_to_float(t_key)

    has_tie = jnp.max(jnp.where((cnt_t == topk) | (tf == NEG_INF), 0, 1))

    @pl.when(has_tie > 0)
    def _():
        t_next = _key_to_float(t_key + 1)
        need = (topk - count_ge(t_next)).astype(F32)
        r = lax.broadcasted_iota(jnp.int32, (QBLK, QBLK), 0)
        c = lax.broadcasted_iota(jnp.int32, (QBLK, QBLK), 1)
        tril = (c <= r).astype(MXU_DTYPE)

        def body(j, carry):
            k0 = pl.multiple_of(j * QBLK, QBLK)
            tile = s_ref[pl.ds(k0, QBLK), :]
            eq = (tile >= tf) & jnp.logical_not(tile >= t_next)
            pre = _dot(tril, eq.astype(MXU_DTYPE)) + carry
            s_ref[pl.ds(k0, QBLK), :] = jnp.where(eq & (pre > need), NEG_INF, tile)
            return pre[QBLK - 1:QBLK, :]

        lax.fori_loop(0, n_tiles, body, jnp.zeros((1, QBLK), F32))

    t_sel = jnp.maximum(tf, -F32_MAX)
    m_ref[...] = jnp.full(m_ref.shape, -1e30, F32)
    l_ref[...] = jnp.zeros(l_ref.shape, F32)
    acc_ref[...] = jnp.zeros(acc_ref.shape, F32)

    def attend(j, bias_of_head):
        k0 = pl.multiple_of(j * QBLK, QBLK)
        ck = ckv_ref[0, pl.ds(k0, QBLK), :]
        ckt = ckvt_ref[0, j]
        sel = s_ref[pl.ds(k0, QBLK), :] >= t_sel
        for hd in range(H_A):
            cs = slice(hd * QBLK, (hd + 1) * QBLK)
            lg = _dot_nt(ck, qlat_ref[0, :, hd * KV_RANK:(hd + 1) * KV_RANK])
            lg = jnp.where(sel, lg + bias_of_head(hd), -1e30)
            m_old = m_ref[:, cs]
            m_new = jnp.maximum(m_old, jnp.max(lg, axis=0, keepdims=True))
            alpha = jnp.exp(m_old - m_new)
            p = jnp.where(sel, jnp.exp(lg - m_new), 0.0)
            l_ref[:, cs] = alpha * l_ref[:, cs] + jnp.sum(p, axis=0, keepdims=True)
            acc_ref[:, cs] = alpha * acc_ref[:, cs] + _dot(ckt, p.astype(MXU_DTYPE))
            m_ref[:, cs] = m_new

    def far_tile(j, carry):
        attend(j, lambda hd: biasf_ref[:, hd * QBLK:(hd + 1) * QBLK])
        return carry

    lax.fori_loop(0, jq - 1, far_tile, 0)

    @pl.when(jq >= 1)
    def _():
        attend(jq - 1, lambda hd: biasn_ref[1, :, hd * QBLK:(hd + 1) * QBLK])

    attend(jq, lambda hd: biasn_ref[0, :, hd * QBLK:(hd + 1) * QBLK])

    outs = []
    for hd in range(H_A):
        cs = slice(hd * QBLK, (hd + 1) * QBLK)
        o_t = acc_ref[:, cs] / l_ref[:, cs]
        outs.append(_dot(o_t.T.astype(MXU_DTYPE), wuv_ref[hd]))
    out_ref[0] = jnp.concatenate(outs, axis=-1).astype(out_ref.dtype)


def _dsa_call(qidx, widxt, qlat, kidx, ckv, ckvt, biasn, biasf, wuv, bsz, s_len):
    topk = min(TOPK_MAX, s_len // 4)
    nblk = s_len // QBLK
    qblock = lambda width: pl.BlockSpec((1, QBLK, width), lambda b, i: (b, i, 0))
    const = lambda shape: pl.BlockSpec(shape, lambda b, i: (0,) * len(shape))
    return pl.pallas_call(
        functools.partial(_dsa_kernel, s_len=s_len, topk=topk),
        grid=(bsz, nblk),
        in_specs=[
            qblock(IDX_HEADS * IDX_DIM),
            pl.BlockSpec((1, SUBLANES, QBLK), lambda b, i: (b, 0, i)),
            qblock(H_A * KV_RANK),
            pl.BlockSpec((1, s_len, IDX_DIM), lambda b, i: (b, 0, 0)),
            pl.BlockSpec((1, s_len, KV_RANK), lambda b, i: (b, 0, 0)),
            pl.BlockSpec((1, nblk, KV_RANK, QBLK), lambda b, i: (b, 0, 0, 0)),
            const((2, QBLK, H_A * QBLK)),
            const((1, H_A * QBLK)),
            const((H_A, KV_RANK, HEAD_DIM)),
        ],
        out_specs=qblock(H_A * HEAD_DIM),
        out_shape=jax.ShapeDtypeStruct((bsz, s_len, H_A * HEAD_DIM), MXU_DTYPE),
        scratch_shapes=[
            pltpu.VMEM((s_len, QBLK), F32),
            pltpu.VMEM((1, H_A * QBLK), F32),
            pltpu.VMEM((1, H_A * QBLK), F32),
            pltpu.VMEM((KV_RANK, H_A * QBLK), F32),
        ],
        compiler_params=pltpu.CompilerParams(
            dimension_semantics=("parallel", "arbitrary"), vmem_limit_bytes=VMEM_LIMIT),
        name="dsa_mixer",
    )(qidx, widxt, qlat, kidx, ckv, ckvt, biasn, biasf, wuv)


def _split3(x):
    a = x.astype(MXU_DTYPE)
    r = x - a.astype(F32)
    b = r.astype(MXU_DTYPE)
    c = (r - b.astype(F32)).astype(MXU_DTYPE)
    return a, b, c


def _gla_kernel(qb_ref, kb_ref, vb_ref, glr_ref, rb_ref, wg_ref, bg_ref, gn_ref,
                out_ref, st_ref):
    @pl.when(pl.program_id(1) == 0)
    def _():
        st_ref[...] = jnp.zeros(st_ref.shape, F32)

    g_hi, g_lo, _ = _split3(glr_ref[0])
    w_hi, w_lo, _ = _split3(wg_ref[...])
    z = _dot(g_hi, w_hi) + (_dot(g_hi, w_lo) + _dot(g_lo, w_hi)) + bg_ref[...]
    g = (jnp.minimum(z, 0.0) - jnp.log1p(jnp.exp(-jnp.abs(z)))) / GATE_TEMP

    r = lax.broadcasted_iota(jnp.int32, (CHUNK, CHUNK), 0)
    c = lax.broadcasted_iota(jnp.int32, (CHUNK, CHUNK), 1)
    causal = c <= r
    tril = causal.astype(MXU_DTYPE)

    for ci in range(TS_GLA // CHUNK):
        rows = slice(ci * CHUNK, (ci + 1) * CHUNK)
        g1, g2, g3 = _split3(g[rows, :])
        b = _dot(tril, g1) + (_dot(tril, g2) + _dot(tril, g3))
        b_last = b[CHUNK - 1:CHUNK, :]
        q = qb_ref[0, rows, :].astype(F32) * (DK_B ** -0.5)
        k = kb_ref[0, rows, :].astype(F32)
        qe = (q * jnp.exp(b)).astype(MXU_DTYPE)
        ke = (k * jnp.exp(-b)).astype(MXU_DTYPE)
        kd = (k * jnp.exp(b_last - b)).astype(MXU_DTYPE)
        decay = jnp.exp(b_last)
        for hd in range(H_B):
            ks = slice(hd * DK_B, (hd + 1) * DK_B)
            vs = slice(hd * DV_B, (hd + 1) * DV_B)
            v = vb_ref[0, rows, vs]
            att = jnp.where(causal, _dot_nt(qe[:, ks], ke[:, ks]), 0.0)
            st = st_ref[hd]
            o = _dot(att.astype(MXU_DTYPE), v) + _dot_nt(qe[:, ks], st.astype(MXU_DTYPE))
            st_ref[hd] = st * decay[:, ks] + _dot_tn(v, kd[:, ks])
            o = o * lax.rsqrt(jnp.mean(o * o, axis=-1, keepdims=True) + EPS) * gn_ref[:, vs]
            rg = rb_ref[0, rows, vs].astype(F32)
            o = o * (rg / (1.0 + jnp.exp(-rg)))
            out_ref[0, rows, vs] = o.astype(out_ref.dtype)


def _gla_call(qb, kb, vb, glr, rb, wg, bg, gn, bsz, s_len):
    ts = TS_GLA
    tok = lambda width: pl.BlockSpec((1, ts, width), lambda b, i: (b, i, 0))
    const = lambda shape: pl.BlockSpec(shape, lambda b, i: (0,) * len(shape))
    return pl.pallas_call(
        _gla_kernel,
        grid=(bsz, s_len // ts),
        in_specs=[tok(H_B * DK_B), tok(H_B * DK_B), tok(H_B * DV_B), tok(LANES), tok(H_B * DV_B),
                  const((LANES, H_B * DK_B)), const((1, H_B * DK_B)), const((1, H_B * DV_B))],
        out_specs=tok(H_B * DV_B),
        out_shape=jax.ShapeDtypeStruct((bsz, s_len, H_B * DV_B), MXU_DTYPE),
        scratch_shapes=[pltpu.VMEM((H_B, DV_B, DK_B), F32)],
        compiler_params=pltpu.CompilerParams(
            dimension_semantics=("parallel", "arbitrary"), vmem_limit_bytes=VMEM_LIMIT),
        name="gla_mixer",
    )(qb, kb, vb, glr, rb, wg, bg, gn)


SWA_WIN = TQ_SWA + WINDOW


def _swa_kernel(sink_ref, qc_ref, kc_ref, vc_ref, mb_ref, out_ref):
    i = pl.program_id(1)
    start = pl.multiple_of(jnp.maximum(i * TQ_SWA - WINDOW, 0), WINDOW)
    k = kc_ref[0, pl.ds(start, SWA_WIN), :]
    v = vc_ref[0, pl.ds(start, SWA_WIN), :]
    grp = H_C // KV_C
    outs = []
    for hd in range(H_C):
        kv = hd // grp
        q = qc_ref[0, :, hd * HEAD_DIM:(hd + 1) * HEAD_DIM]
        lg = _dot_nt(q, k[:, kv * HEAD_DIM:(kv + 1) * HEAD_DIM]) * (HEAD_DIM ** -0.5) + mb_ref[0, hd]
        sink = sink_ref[hd]
        m = jnp.maximum(jnp.max(lg, axis=-1, keepdims=True), sink)
        e = jnp.exp(lg - m)
        p = e / (jnp.sum(e, axis=-1, keepdims=True) + jnp.exp(sink - m))
        outs.append(_dot(p.astype(MXU_DTYPE), v[:, kv * HEAD_DIM:(kv + 1) * HEAD_DIM]))
    out_ref[0] = jnp.concatenate(outs, axis=-1).astype(out_ref.dtype)


def _swa_call(sinks, qc, kc, vc, maskbias, bsz, s_len):
    tq = TQ_SWA
    return pl.pallas_call(
        _swa_kernel,
        grid=(bsz, s_len // tq),
        in_specs=[
            pl.BlockSpec(memory_space=pltpu.SMEM),
            pl.BlockSpec((1, tq, H_C * HEAD_DIM), lambda b, i: (b, i, 0)),
            pl.BlockSpec((1, s_len, KV_C * HEAD_DIM), lambda b, i: (b, 0, 0)),
            pl.BlockSpec((1, s_len, KV_C * HEAD_DIM), lambda b, i: (b, 0, 0)),
            pl.BlockSpec((1, H_C, tq, SWA_WIN), lambda b, i: (jnp.minimum(i, 1), 0, 0, 0)),
        ],
        out_specs=pl.BlockSpec((1, tq, H_C * HEAD_DIM), lambda b, i: (b, i, 0)),
        out_shape=jax.ShapeDtypeStruct((bsz, s_len, H_C * HEAD_DIM), MXU_DTYPE),
        compiler_params=pltpu.CompilerParams(
            dimension_semantics=("parallel", "arbitrary"), vmem_limit_bytes=VMEM_LIMIT),
        name="swa_mixer",
    )(sinks, qc, kc, vc, maskbias)


def _ffn_kernel(x_ref, oa_ref, ob_ref, oc_ref, wo_ref, gf_ref, wg_ref, wu_ref, wd_ref,
                gfin_ref, out_ref, x1_ref, h_ref, acc_ref, *, final_norm):
    j = pl.program_id(1)
    na = H_A * HEAD_DIM
    nb = H_B * DV_B

    @pl.when(j == 0)
    def _():
        x1 = (x_ref[...] + _dot(oa_ref[...], wo_ref[0:na, :])
              + _dot(ob_ref[...], wo_ref[na:na + nb, :])
              + _dot(oc_ref[...], wo_ref[na + nb:, :]))
        x1_ref[...] = x1
        h_ref[...] = _rms(x1, gf_ref[...]).astype(h_ref.dtype)
        acc_ref[...] = jnp.zeros(acc_ref.shape, F32)

    h = h_ref[...]
    gate = _dot(h, wg_ref[...])
    up = _dot(h, wu_ref[...])
    a = (gate / (1.0 + jnp.exp(-gate))) * up
    acc_ref[...] += _dot(a.astype(MXU_DTYPE), wd_ref[...])

    @pl.when(j == pl.num_programs(1) - 1)
    def _():
        y = x1_ref[...] + acc_ref[...]
        if final_norm:
            y = _rms(y, gfin_ref[...])
        out_ref[...] = y


def _ffn_call(x2, oa, ob, oc, wo, gf, wg, wu, wd, gfin, final_norm):
    t_tok = x2.shape[0]
    tm, tf = TM_FFN, TF_FFN
    tok = lambda width: pl.BlockSpec((tm, width), lambda i, j: (i, 0))
    const = lambda shape: pl.BlockSpec(shape, lambda i, j: (0,) * len(shape))
    return pl.pallas_call(
        functools.partial(_ffn_kernel, final_norm=final_norm),
        grid=(t_tok // tm, D_FF // tf),
        in_specs=[tok(D_MODEL), tok(H_A * HEAD_DIM), tok(H_B * DV_B), tok(H_C * HEAD_DIM),
                  const((D_MODEL, D_MODEL)), const((1, D_MODEL)),
                  pl.BlockSpec((D_MODEL, tf), lambda i, j: (0, j)),
                  pl.BlockSpec((D_MODEL, tf), lambda i, j: (0, j)),
                  pl.BlockSpec((tf, D_MODEL), lambda i, j: (j, 0)),
                  const((1, D_MODEL))],
        out_specs=tok(D_MODEL),
        out_shape=jax.ShapeDtypeStruct((t_tok, D_MODEL), F32),
        scratch_shapes=[pltpu.VMEM((tm, D_MODEL), F32),
                        pltpu.VMEM((tm, D_MODEL), MXU_DTYPE),
                        pltpu.VMEM((tm, D_MODEL), F32)],
        compiler_params=pltpu.CompilerParams(
            dimension_semantics=("parallel", "arbitrary"), vmem_limit_bytes=VMEM_LIMIT),
        name="out_proj_ffn",
    )(x2, oa, ob, oc, wo, gf, wg, wu, wd, gfin)


def _rel_bucket(rel):
    nb = NUM_BUCKETS // 2
    max_exact = nb // 2
    n = jnp.abs(rel)
    side = jnp.where(rel > 0, nb, 0)
    nf = jnp.maximum(n, 1).astype(jnp.float32)
    large = max_exact + (jnp.log(nf / max_exact) / math.log(MAX_DISTANCE / max_exact)
                         * (nb - max_exact)).astype(jnp.int32)
    large = jnp.minimum(large, nb - 1)
    return side + jnp.where(n < max_exact, n, large)


def _dsa_bias_tables(rel_bias):
    table = rel_bias[:, :H_A].astype(F32)
    kk = jnp.arange(QBLK, dtype=jnp.int32)[:, None]
    qq = jnp.arange(QBLK, dtype=jnp.int32)[None, :]
    near = []
    for shift in (0, QBLK):
        b = table[_rel_bucket(kk - shift - qq)]
        near.append(jnp.transpose(b, (0, 2, 1)).reshape(QBLK, H_A * QBLK))
    far = table[_rel_bucket(jnp.full((1,), -2 * QBLK, jnp.int32))]
    far = jnp.repeat(far, QBLK, axis=1)
    return jnp.stack(near), far


def _swa_maskbias(rel_bias):
    table = rel_bias[:, H_A:].astype(F32)
    qq = jnp.arange(TQ_SWA, dtype=jnp.int32)[:, None]
    out = []
    for start in (0, -WINDOW):
        kk = start + jnp.arange(SWA_WIN, dtype=jnp.int32)[None, :]
        dc = qq // CHUNK - jnp.floor_divide(kk, CHUNK)
        band = (dc >= 0) & (dc <= WIN_CHUNKS)
        b = jnp.transpose(table[_rel_bucket(kk - qq)], (2, 0, 1))
        out.append(jnp.where(band[None], b, NEG_INF))
    return jnp.stack(out)


def kernel(x, w_in, w_out, norm_mix, norm_ffn, kv_norm, w_uk, w_uv, w_gate2, b_gate,
           gla_norm, sinks, rel_bias, w_ffn_gate, w_ffn_up, w_ffn_down, final_norm):
    bsz, s_len, _ = x.shape
    t_tok = bsz * s_len
    wd = MXU_DTYPE
    biasn, biasf = _dsa_bias_tables(rel_bias)
    maskbias = _swa_maskbias(rel_bias)
    x2 = x.reshape(t_tok, D_MODEL)
    row = lambda v: v.reshape(1, -1).astype(F32)
    for l in range(DEPTH):
        w_arr = _arrange_w_in(w_in[l]).astype(wd)
        wuk = jnp.transpose(w_uk[l], (1, 2, 0)).astype(wd)
        wuv = jnp.transpose(w_uv[l], (1, 0, 2)).astype(wd)
        (qlat, ckv, ckvt, qidx, kidx, widxt, qb, kb, vb, glr, rb, qc, kc, vc) = _proj_call(
            x2, row(norm_mix[l]), w_arr, wuk, row(kv_norm[l]), bsz, s_len)
        b3 = lambda a: a.reshape(bsz, s_len, a.shape[-1])
        out_a = _dsa_call(b3(qidx), widxt, b3(qlat), b3(kidx), b3(ckv), ckvt,
                          biasn, biasf, wuv, bsz, s_len)
        wg2 = jnp.zeros((LANES, H_B * DK_B), F32).at[:GATE_RANK].set(w_gate2[l])
        out_b = _gla_call(b3(qb), b3(kb), b3(vb), b3(glr), b3(rb), wg2,
                          row(b_gate[l]), row(gla_norm[l]), bsz, s_len)
        out_c = _swa_call(sinks[l].astype(F32), b3(qc), b3(kc), b3(vc), maskbias, bsz, s_len)
        f2 = lambda a: a.reshape(t_tok, a.shape[-1])
        x2 = _ffn_call(x2, f2(out_a), f2(out_b), f2(out_c), w_out[l].astype(wd),
                       row(norm_ffn[l]), w_ffn_gate[l].astype(wd), w_ffn_up[l].astype(wd),
                       w_ffn_down[l].astype(wd), row(final_norm), l == DEPTH - 1)
    return x2.reshape(bsz, s_len, D_MODEL)
```

```python
import functools
import math

import jax
import jax.numpy as jnp
from jax import lax
from jax.experimental import pallas as pl
from jax.experimental.pallas import tpu as pltpu

D_MODEL = 1024
DEPTH = 2
CHUNK = 64
HEAD_DIM = 64
H_A = 4
KV_RANK = 128
IDX_HEADS = 4
IDX_DIM = 64
TOPK_MAX = 256
H_B = 4
DK_B = 64
DV_B = 128
GATE_RANK = 16
GATE_TEMP = 16.0
H_C = 4
KV_C = 2
WINDOW = 128
WIN_CHUNKS = WINDOW // CHUNK
NUM_BUCKETS = 32
MAX_DISTANCE = 128
D_FF = ((8 * D_MODEL // 3 + 255) // 256) * 256
EPS = 1e-6

IN_SPLITS = (H_A * HEAD_DIM, KV_RANK, IDX_HEADS * IDX_DIM, IDX_DIM, IDX_HEADS,
             H_B * DK_B, H_B * DK_B, H_B * DV_B, GATE_RANK, H_B * DV_B,
             H_C * HEAD_DIM, KV_C * HEAD_DIM, KV_C * HEAD_DIM)

LANES = 128
SUBLANES = 8
VMEM_LIMIT = 56 * 1024 * 1024

MXU_DTYPE = jnp.bfloat16
F32 = jnp.float32
NEG_INF = float("-inf")
MASKED = -1e30
F32_MAX = float(jnp.finfo(jnp.float32).max)
INT_MIN = -2 ** 31

QBLK = 256
TM_PROJ = 512
TS_GLA = 256
TQ_SWA = 256
TM_FFN = 512
TF_FFN = D_FF // 2


def _dot(a, b):
    return jnp.dot(a, b, preferred_element_type=F32)


def _dot_nt(a, b):
    return lax.dot_general(a, b, (((1,), (1,)), ((), ())), preferred_element_type=F32)


def _dot_tn(a, b):
    return lax.dot_general(a, b, (((0,), (0,)), ((), ())), preferred_element_type=F32)


def _rms(x, g):
    return x * lax.rsqrt(jnp.mean(x * x, axis=-1, keepdims=True) + EPS) * g


_C_QA = 0
_C_CKV = 256
_C_QIDX = 384
_C_KW = 640
_C_QB = 768
_C_KB = 1024
_C_VB = 1280
_C_GLR = 1792
_C_RB = 1920
_C_QC = 2432
_C_KC = 2688
_C_VC = 2816
_NP = 2944


def _proj_kernel(x_ref, g_ref, w_ref, wuk_ref, kvn_ref,
                 qlat_ref, ckv_ref, ckvt_ref, qidx_ref, kidx_ref, widxt_ref,
                 qb_ref, kb_ref, vb_ref, glr_ref, rb_ref, qc_ref, kc_ref, vc_ref):
    h = _rms(x_ref[...], g_ref[...]).astype(MXU_DTYPE)

    def seg(c0, width):
        return _dot(h, w_ref[:, c0:c0 + width])

    qa = seg(_C_QA, 256)
    for hd in range(H_A):
        qh = qa[:, hd * HEAD_DIM:(hd + 1) * HEAD_DIM].astype(MXU_DTYPE)
        ql = _dot(qh, wuk_ref[hd]) * (HEAD_DIM ** -0.5)
        for t in range(TM_PROJ // QBLK):
            qlat_ref[0, t, :, hd * QBLK:(hd + 1) * QBLK] = (
                ql[t * QBLK:(t + 1) * QBLK, :].T.astype(qlat_ref.dtype))

    ckv = _rms(seg(_C_CKV, KV_RANK), kvn_ref[...])
    ckv_ref[...] = ckv.astype(ckv_ref.dtype)
    for t in range(TM_PROJ // QBLK):
        ckvt_ref[0, t] = ckv[t * QBLK:(t + 1) * QBLK, :].T.astype(ckvt_ref.dtype)

    qi = seg(_C_QIDX, 256)
    for t in range(TM_PROJ // QBLK):
        qit = qi[t * QBLK:(t + 1) * QBLK, :].T
        for hd in range(IDX_HEADS):
            qidx_ref[0, t, :, hd * QBLK:(hd + 1) * QBLK] = (
                qit[hd * IDX_DIM:(hd + 1) * IDX_DIM, :].astype(qidx_ref.dtype))
    kw = seg(_C_KW, LANES)
    kidx_ref[...] = kw[:, :IDX_DIM].astype(kidx_ref.dtype)
    kwt = kw.T
    widxt_ref[0] = kwt[IDX_DIM:IDX_DIM + SUBLANES, :] * (IDX_HEADS ** -0.5 * IDX_DIM ** -0.5)

    qb_ref[...] = seg(_C_QB, 256).astype(qb_ref.dtype)
    kb_ref[...] = seg(_C_KB, 256).astype(kb_ref.dtype)
    vb_ref[...] = seg(_C_VB, 512).astype(vb_ref.dtype)
    glr_ref[...] = seg(_C_GLR, LANES)
    rb_ref[...] = seg(_C_RB, 512).astype(rb_ref.dtype)
    qc_ref[...] = seg(_C_QC, 256).astype(qc_ref.dtype)
    kc_ref[...] = seg(_C_KC, LANES).astype(kc_ref.dtype)
    vc_ref[...] = seg(_C_VC, LANES).astype(vc_ref.dtype)


def _arrange_w_in(w):
    offs = [0]
    for s in IN_SPLITS:
        offs.append(offs[-1] + s)
    parts = [w[:, offs[i]:offs[i + 1]] for i in range(len(IN_SPLITS))]
    (qa, ckv, qidx, kidx, widx, qb, kb, vb, glr, rb, qc, kc, vc) = parts
    z = lambda n: jnp.zeros((w.shape[0], n), w.dtype)
    return jnp.concatenate(
        [qa, ckv, qidx, kidx, widx, z(LANES - IDX_DIM - IDX_HEADS), qb, kb, vb,
         glr, z(LANES - GATE_RANK), rb, qc, kc, vc], axis=1)


def _proj_call(x2, g, w, wuk, kvn, bsz, s_len):
    t_tok = x2.shape[0]
    tm = TM_PROJ
    n_s = s_len // tm
    act = MXU_DTYPE

    def tok(width):
        return pl.BlockSpec((tm, width), lambda i: (i, 0))

    def const(shape):
        return pl.BlockSpec(shape, lambda i: (0,) * len(shape))

    def qblocks(depth):
        return pl.BlockSpec((1, tm // QBLK, depth, H_A * QBLK), lambda i: (i // n_s, i % n_s, 0, 0))

    out_shape = [
        jax.ShapeDtypeStruct((bsz, s_len // QBLK, KV_RANK, H_A * QBLK), act),
        jax.ShapeDtypeStruct((t_tok, KV_RANK), act),
        jax.ShapeDtypeStruct((bsz, s_len // QBLK, KV_RANK, QBLK), act),
        jax.ShapeDtypeStruct((bsz, s_len // QBLK, IDX_DIM, IDX_HEADS * QBLK), act),
        jax.ShapeDtypeStruct((t_tok, IDX_DIM), act),
        jax.ShapeDtypeStruct((bsz, SUBLANES, s_len), F32),
        jax.ShapeDtypeStruct((t_tok, H_B * DK_B), act),
        jax.ShapeDtypeStruct((t_tok, H_B * DK_B), act),
        jax.ShapeDtypeStruct((t_tok, H_B * DV_B), act),
        jax.ShapeDtypeStruct((t_tok, LANES), F32),
        jax.ShapeDtypeStruct((t_tok, H_B * DV_B), act),
        jax.ShapeDtypeStruct((t_tok, H_C * HEAD_DIM), act),
        jax.ShapeDtypeStruct((t_tok, KV_C * HEAD_DIM), act),
        jax.ShapeDtypeStruct((t_tok, KV_C * HEAD_DIM), act),
    ]
    out_specs = [
        qblocks(KV_RANK), tok(KV_RANK),
        pl.BlockSpec((1, tm // QBLK, KV_RANK, QBLK), lambda i: (i // n_s, i % n_s, 0, 0)),
        qblocks(IDX_DIM), tok(IDX_DIM),
        pl.BlockSpec((1, SUBLANES, tm), lambda i: (i // n_s, 0, i % n_s)),
        tok(H_B * DK_B), tok(H_B * DK_B), tok(H_B * DV_B), tok(LANES), tok(H_B * DV_B),
        tok(H_C * HEAD_DIM), tok(KV_C * HEAD_DIM), tok(KV_C * HEAD_DIM),
    ]
    return pl.pallas_call(
        _proj_kernel,
        grid=(t_tok // tm,),
        in_specs=[tok(D_MODEL), const((1, D_MODEL)), const((D_MODEL, _NP)),
                  const((H_A, HEAD_DIM, KV_RANK)), const((1, KV_RANK))],
        out_specs=out_specs,
        out_shape=out_shape,
        compiler_params=pltpu.CompilerParams(
            dimension_semantics=("parallel",), vmem_limit_bytes=VMEM_LIMIT),
        name="norm_in_proj",
    )(x2, g, w, wuk, kvn)


def _key_to_float(k):
    bits = jnp.where(k >= 0, k, INT_MIN - k)
    return lax.bitcast_convert_type(bits, F32)


def _dsa_kernel(qidx_ref, widxt_ref, qlat_ref, kidx_ref, ckv_ref, ckvt_ref,
                biasn_ref, biasf_ref, wuv_ref, out_ref,
                s_ref, lg_ref, acc_ref, *, s_len, topk):
    jq = pl.program_id(1)
    n_tiles = jq + 1
    n_virtual = s_len - n_tiles * QBLK
    hq = H_A * QBLK
    head_cols = [slice(hd * QBLK, (hd + 1) * QBLK) for hd in range(H_A)]

    def rows(j):
        return pl.ds(pl.multiple_of(j * QBLK, QBLK), QBLK)

    def fold(x, op):
        return op(x.reshape(QBLK // SUBLANES, SUBLANES, x.shape[-1]), axis=0)

    qis = qidx_ref[0, 0]
    wt = widxt_ref[0]

    def scores(j):
        d = _dot(kidx_ref[0, rows(j), :], qis)
        s = jnp.maximum(d[:, head_cols[0]], 0.0) * wt[0:1, :]
        for hd in range(1, IDX_HEADS):
            s = s + jnp.maximum(d[:, head_cols[hd]], 0.0) * wt[hd:hd + 1, :]
        return s

    def full_tile(j, carry):
        s_ref[rows(j), :] = scores(j)
        return carry

    lax.fori_loop(0, jq, full_tile, 0)
    kchunk = lax.broadcasted_iota(jnp.int32, (QBLK, 1), 0) // CHUNK
    qchunk = lax.broadcasted_iota(jnp.int32, (1, QBLK), 1) // CHUNK
    s_ref[rows(jq), :] = jnp.where(kchunk <= qchunk, scores(jq), NEG_INF)

    def count_ge(tf):
        def tile_count(j):
            return fold(jnp.where(s_ref[rows(j), :] >= tf, 1, 0), jnp.sum)

        c8 = lax.fori_loop(0, n_tiles // 2,
                           lambda i, c8: c8 + (tile_count(2 * i) + tile_count(2 * i + 1)),
                           jnp.zeros((SUBLANES, QBLK), jnp.int32))
        c8 = lax.cond(n_tiles % 2 == 1, lambda c8: c8 + tile_count(n_tiles - 1), lambda c8: c8, c8)
        virt = jnp.where(NEG_INF >= tf, n_virtual, 0)
        return jnp.sum(c8, axis=0, keepdims=True) + virt

    def bit_step(i, carry):
        t_key, cnt_t = carry
        cand = t_key + jnp.left_shift(jnp.int32(1), 31 - i)
        cnt = count_ge(_key_to_float(cand))
        ok = cnt >= topk
        return jnp.where(ok, cand, t_key), jnp.where(ok, cnt, cnt_t)

    t_key, cnt_t = lax.fori_loop(
        0, 32, bit_step,
        (jnp.full((1, QBLK), INT_MIN, jnp.int32), jnp.full((1, QBLK), s_len, jnp.int32)))
    tf = _key_to_float(t_key)

    has_tie = jnp.max(jnp.where((cnt_t == topk) | (tf == NEG_INF), 0, 1))

    @pl.when(has_tie > 0)
    def _():
        t_next = _key_to_float(t_key + 1)
        need = (topk - count_ge(t_next)).astype(F32)
        r = lax.broadcasted_iota(jnp.int32, (QBLK, QBLK), 0)
        c = lax.broadcasted_iota(jnp.int32, (QBLK, QBLK), 1)
        tril = (c <= r).astype(MXU_DTYPE)

        def body(j, carry):
            tile = s_ref[rows(j), :]
            eq = (tile >= tf) & jnp.logical_not(tile >= t_next)
            pre = _dot(tril, eq.astype(MXU_DTYPE)) + carry
            s_ref[rows(j), :] = jnp.where(eq & (pre > need), NEG_INF, tile)
            return pre[QBLK - 1:QBLK, :]

        lax.fori_loop(0, n_tiles, body, jnp.zeros((1, QBLK), F32))

    t_sel = jnp.maximum(tf, -F32_MAX)
    qls = qlat_ref[0, 0]

    def pass_a(j, bias_of_head, mx):
        lg = _dot(ckv_ref[0, rows(j), :], qls)
        sel = s_ref[rows(j), :] >= t_sel
        parts = []
        for hd in range(H_A):
            v = jnp.where(sel, lg[:, head_cols[hd]] + bias_of_head(hd), MASKED)
            lg_ref[rows(j), head_cols[hd]] = v
            parts.append(fold(v, jnp.max))
        return jnp.maximum(mx, jnp.concatenate(parts, axis=-1))

    far_bias = lambda hd: biasf_ref[:, head_cols[hd]]
    mx = lax.fori_loop(0, jq - 1, lambda j, mx: pass_a(j, far_bias, mx),
                       jnp.full((SUBLANES, hq), MASKED, F32))
    mx = lax.cond(jq >= 1,
                  lambda mx: pass_a(jq - 1, lambda hd: biasn_ref[1, :, head_cols[hd]], mx),
                  lambda mx: mx, mx)
    mx = pass_a(jq, lambda hd: biasn_ref[0, :, head_cols[hd]], mx)
    m = jnp.max(mx, axis=0, keepdims=True)

    acc_ref[...] = jnp.zeros(acc_ref.shape, F32)

    def pass_b(j, l8):
        ckt = ckvt_ref[0, j]
        parts = []
        for hd in range(H_A):
            p = jnp.exp(lg_ref[rows(j), head_cols[hd]] - m[:, head_cols[hd]])
            acc_ref[:, head_cols[hd]] += _dot(ckt, p.astype(MXU_DTYPE))
            parts.append(fold(p, jnp.sum))
        return l8 + jnp.concatenate(parts, axis=-1)

    l8 = lax.fori_loop(0, n_tiles, pass_b, jnp.zeros((SUBLANES, hq), F32))
    l = jnp.sum(l8, axis=0, keepdims=True)

    outs = []
    for hd in range(H_A):
        o_t = acc_ref[:, head_cols[hd]] / l[:, head_cols[hd]]
        outs.append(_dot(o_t.T.astype(MXU_DTYPE), wuv_ref[hd]))
    out_ref[0] = jnp.concatenate(outs, axis=-1).astype(out_ref.dtype)


def _dsa_call(qidx, widxt, qlat, kidx, ckv, ckvt, biasn, biasf, wuv, bsz, s_len):
    topk = min(TOPK_MAX, s_len // 4)
    nblk = s_len // QBLK
    qblock_t = lambda depth: pl.BlockSpec((1, 1, depth, H_A * QBLK), lambda b, i: (b, i, 0, 0))
    const = lambda shape: pl.BlockSpec(shape, lambda b, i: (0,) * len(shape))
    return pl.pallas_call(
        functools.partial(_dsa_kernel, s_len=s_len, topk=topk),
        grid=(bsz, nblk),
        in_specs=[
            qblock_t(IDX_DIM),
            pl.BlockSpec((1, SUBLANES, QBLK), lambda b, i: (b, 0, i)),
            qblock_t(KV_RANK),
            pl.BlockSpec((1, s_len, IDX_DIM), lambda b, i: (b, 0, 0)),
            pl.BlockSpec((1, s_len, KV_RANK), lambda b, i: (b, 0, 0)),
            pl.BlockSpec((1, nblk, KV_RANK, QBLK), lambda b, i: (b, 0, 0, 0)),
            const((2, QBLK, H_A * QBLK)),
            const((1, H_A * QBLK)),
            const((H_A, KV_RANK, HEAD_DIM)),
        ],
        out_specs=pl.BlockSpec((1, QBLK, H_A * HEAD_DIM), lambda b, i: (b, i, 0)),
        out_shape=jax.ShapeDtypeStruct((bsz, s_len, H_A * HEAD_DIM), MXU_DTYPE),
        scratch_shapes=[
            pltpu.VMEM((s_len, QBLK), F32),
            pltpu.VMEM((s_len, H_A * QBLK), F32),
            pltpu.VMEM((KV_RANK, H_A * QBLK), F32),
        ],
        compiler_params=pltpu.CompilerParams(
            dimension_semantics=("parallel", "arbitrary"), vmem_limit_bytes=VMEM_LIMIT),
        name="dsa_mixer",
    )(qidx, widxt, qlat, kidx, ckv, ckvt, biasn, biasf, wuv)


def _split3(x):
    a = x.astype(MXU_DTYPE)
    r = x - a.astype(F32)
    b = r.astype(MXU_DTYPE)
    c = (r - b.astype(F32)).astype(MXU_DTYPE)
    return a, b, c


def _gla_kernel(qb_ref, kb_ref, vb_ref, glr_ref, rb_ref, wg_ref, bg_ref, gn_ref,
                out_ref, st_ref):
    @pl.when(pl.program_id(1) == 0)
    def _():
        st_ref[...] = jnp.zeros(st_ref.shape, F32)

    g_hi, g_lo, _ = _split3(glr_ref[0])
    w_hi, w_lo, _ = _split3(wg_ref[...])
    z = _dot(g_hi, w_hi) + (_dot(g_hi, w_lo) + _dot(g_lo, w_hi)) + bg_ref[...]
    g = (jnp.minimum(z, 0.0) - jnp.log1p(jnp.exp(-jnp.abs(z)))) / GATE_TEMP

    r = lax.broadcasted_iota(jnp.int32, (CHUNK, CHUNK), 0)
    c = lax.broadcasted_iota(jnp.int32, (CHUNK, CHUNK), 1)
    causal = c <= r
    tril = causal.astype(MXU_DTYPE)

    for ci in range(TS_GLA // CHUNK):
        rows = slice(ci * CHUNK, (ci + 1) * CHUNK)
        g1, g2, g3 = _split3(g[rows, :])
        b = _dot(tril, g1) + (_dot(tril, g2) + _dot(tril, g3))
        b_last = b[CHUNK - 1:CHUNK, :]
        q = qb_ref[0, rows, :].astype(F32) * (DK_B ** -0.5)
        k = kb_ref[0, rows, :].astype(F32)
        qe = (q * jnp.exp(b)).astype(MXU_DTYPE)
        ke = (k * jnp.exp(-b)).astype(MXU_DTYPE)
        kd = (k * jnp.exp(b_last - b)).astype(MXU_DTYPE)
        decay = jnp.exp(b_last)
        for hd in range(H_B):
            ks = slice(hd * DK_B, (hd + 1) * DK_B)
            vs = slice(hd * DV_B, (hd + 1) * DV_B)
            v = vb_ref[0, rows, vs]
            att = jnp.where(causal, _dot_nt(qe[:, ks], ke[:, ks]), 0.0)
            st = st_ref[hd]
            o = _dot(att.astype(MXU_DTYPE), v) + _dot_nt(qe[:, ks], st.astype(MXU_DTYPE))
            st_ref[hd] = st * decay[:, ks] + _dot_tn(v, kd[:, ks])
            o = o * lax.rsqrt(jnp.mean(o * o, axis=-1, keepdims=True) + EPS) * gn_ref[:, vs]
            rg = rb_ref[0, rows, vs].astype(F32)
            o = o * (rg / (1.0 + jnp.exp(-rg)))
            out_ref[0, rows, vs] = o.astype(out_ref.dtype)


def _gla_call(qb, kb, vb, glr, rb, wg, bg, gn, bsz, s_len):
    ts = TS_GLA
    tok = lambda width: pl.BlockSpec((1, ts, width), lambda b, i: (b, i, 0))
    const = lambda shape: pl.BlockSpec(shape, lambda b, i: (0,) * len(shape))
    return pl.pallas_call(
        _gla_kernel,
        grid=(bsz, s_len // ts),
        in_specs=[tok(H_B * DK_B), tok(H_B * DK_B), tok(H_B * DV_B), tok(LANES), tok(H_B * DV_B),
                  const((LANES, H_B * DK_B)), const((1, H_B * DK_B)), const((1, H_B * DV_B))],
        out_specs=tok(H_B * DV_B),
        out_shape=jax.ShapeDtypeStruct((bsz, s_len, H_B * DV_B), MXU_DTYPE),
        scratch_shapes=[pltpu.VMEM((H_B, DV_B, DK_B), F32)],
        compiler_params=pltpu.CompilerParams(
            dimension_semantics=("parallel", "arbitrary"), vmem_limit_bytes=VMEM_LIMIT),
        name="gla_mixer",
    )(qb, kb, vb, glr, rb, wg, bg, gn)


SWA_WIN = TQ_SWA + WINDOW


def _swa_kernel(sink_ref, qc_ref, kc_ref, vc_ref, mb_ref, out_ref):
    i = pl.program_id(1)
    start = pl.multiple_of(jnp.maximum(i * TQ_SWA - WINDOW, 0), WINDOW)
    k = kc_ref[0, pl.ds(start, SWA_WIN), :]
    v = vc_ref[0, pl.ds(start, SWA_WIN), :]
    grp = H_C // KV_C
    outs = []
    for hd in range(H_C):
        kv = hd // grp
        q = qc_ref[0, :, hd * HEAD_DIM:(hd + 1) * HEAD_DIM]
        lg = _dot_nt(q, k[:, kv * HEAD_DIM:(kv + 1) * HEAD_DIM]) * (HEAD_DIM ** -0.5) + mb_ref[0, hd]
        sink = sink_ref[hd]
        m = jnp.maximum(jnp.max(lg, axis=-1, keepdims=True), sink)
        e = jnp.exp(lg - m)
        p = e / (jnp.sum(e, axis=-1, keepdims=True) + jnp.exp(sink - m))
        outs.append(_dot(p.astype(MXU_DTYPE), v[:, kv * HEAD_DIM:(kv + 1) * HEAD_DIM]))
    out_ref[0] = jnp.concatenate(outs, axis=-1).astype(out_ref.dtype)


def _swa_call(sinks, qc, kc, vc, maskbias, bsz, s_len):
    tq = TQ_SWA
    return pl.pallas_call(
        _swa_kernel,
        grid=(bsz, s_len // tq),
        in_specs=[
            pl.BlockSpec(memory_space=pltpu.SMEM),
            pl.BlockSpec((1, tq, H_C * HEAD_DIM), lambda b, i: (b, i, 0)),
            pl.BlockSpec((1, s_len, KV_C * HEAD_DIM), lambda b, i: (b, 0, 0)),
            pl.BlockSpec((1, s_len, KV_C * HEAD_DIM), lambda b, i: (b, 0, 0)),
            pl.BlockSpec((1, H_C, tq, SWA_WIN), lambda b, i: (jnp.minimum(i, 1), 0, 0, 0)),
        ],
        out_specs=pl.BlockSpec((1, tq, H_C * HEAD_DIM), lambda b, i: (b, i, 0)),
        out_shape=jax.ShapeDtypeStruct((bsz, s_len, H_C * HEAD_DIM), MXU_DTYPE),
        compiler_params=pltpu.CompilerParams(
            dimension_semantics=("parallel", "arbitrary"), vmem_limit_bytes=VMEM_LIMIT),
        name="swa_mixer",
    )(sinks, qc, kc, vc, maskbias)


def _ffn_kernel(x_ref, oa_ref, ob_ref, oc_ref, wo_ref, gf_ref, wg_ref, wu_ref, wd_ref,
                gfin_ref, out_ref, x1_ref, h_ref, acc_ref, *, final_norm):
    j = pl.program_id(1)
    na = H_A * HEAD_DIM
    nb = H_B * DV_B

    @pl.when(j == 0)
    def _():
        x1 = (x_ref[...] + _dot(oa_ref[...], wo_ref[0:na, :])
              + _dot(ob_ref[...], wo_ref[na:na + nb, :])
              + _dot(oc_ref[...], wo_ref[na + nb:, :]))
        x1_ref[...] = x1
        h_ref[...] = _rms(x1, gf_ref[...]).astype(h_ref.dtype)
        acc_ref[...] = jnp.zeros(acc_ref.shape, F32)

    h = h_ref[...]
    gate = _dot(h, wg_ref[...])
    up = _dot(h, wu_ref[...])
    a = (gate / (1.0 + jnp.exp(-gate))) * up
    acc_ref[...] += _dot(a.astype(MXU_DTYPE), wd_ref[...])

    @pl.when(j == pl.num_programs(1) - 1)
    def _():
        y = x1_ref[...] + acc_ref[...]
        if final_norm:
            y = _rms(y, gfin_ref[...])
        out_ref[...] = y


def _ffn_call(x2, oa, ob, oc, wo, gf, wg, wu, wd, gfin, final_norm):
    t_tok = x2.shape[0]
    tm, tf = TM_FFN, TF_FFN
    tok = lambda width: pl.BlockSpec((tm, width), lambda i, j: (i, 0))
    const = lambda shape: pl.BlockSpec(shape, lambda i, j: (0,) * len(shape))
    return pl.pallas_call(
        functools.partial(_ffn_kernel, final_norm=final_norm),
        grid=(t_tok // tm, D_FF // tf),
        in_specs=[tok(D_MODEL), tok(H_A * HEAD_DIM), tok(H_B * DV_B), tok(H_C * HEAD_DIM),
                  const((D_MODEL, D_MODEL)), const((1, D_MODEL)),
                  pl.BlockSpec((D_MODEL, tf), lambda i, j: (0, j)),
                  pl.BlockSpec((D_MODEL, tf), lambda i, j: (0, j)),
                  pl.BlockSpec((tf, D_MODEL), lambda i, j: (j, 0)),
                  const((1, D_MODEL))],
        out_specs=tok(D_MODEL),
        out_shape=jax.ShapeDtypeStruct((t_tok, D_MODEL), F32),
        scratch_shapes=[pltpu.VMEM((tm, D_MODEL), F32),
                        pltpu.VMEM((tm, D_MODEL), MXU_DTYPE),
                        pltpu.VMEM((tm, D_MODEL), F32)],
        compiler_params=pltpu.CompilerParams(
            dimension_semantics=("parallel", "arbitrary"), vmem_limit_bytes=VMEM_LIMIT),
        name="out_proj_ffn",
    )(x2, oa, ob, oc, wo, gf, wg, wu, wd, gfin)


def _rel_bucket(rel):
    nb = NUM_BUCKETS // 2
    max_exact = nb // 2
    n = jnp.abs(rel)
    side = jnp.where(rel > 0, nb, 0)
    nf = jnp.maximum(n, 1).astype(jnp.float32)
    large = max_exact + (jnp.log(nf / max_exact) / math.log(MAX_DISTANCE / max_exact)
                         * (nb - max_exact)).astype(jnp.int32)
    large = jnp.minimum(large, nb - 1)
    return side + jnp.where(n < max_exact, n, large)


def _lookup(table, bucket):
    ids = jnp.arange(NUM_BUCKETS, dtype=jnp.int32)[:, None]
    return jnp.sum(jnp.where(bucket[..., None, None] == ids, table, 0.0), axis=-2)


def _dsa_bias_tables(rel_bias):
    table = rel_bias[:, :H_A].astype(F32)
    kk = jnp.arange(QBLK, dtype=jnp.int32)[:, None]
    qq = jnp.arange(QBLK, dtype=jnp.int32)[None, :]
    near = []
    for shift in (0, QBLK):
        b = _lookup(table, _rel_bucket(kk - shift - qq))
        near.append(jnp.transpose(b, (0, 2, 1)).reshape(QBLK, H_A * QBLK))
    far = _lookup(table, _rel_bucket(jnp.full((1,), -2 * QBLK, jnp.int32)))
    far = jnp.repeat(far, QBLK, axis=1)
    return jnp.stack(near), far


def _swa_maskbias(rel_bias):
    table = rel_bias[:, H_A:].astype(F32)
    qq = jnp.arange(TQ_SWA, dtype=jnp.int32)[:, None]
    out = []
    for start in (0, -WINDOW):
        kk = start + jnp.arange(SWA_WIN, dtype=jnp.int32)[None, :]
        dc = qq // CHUNK - jnp.floor_divide(kk, CHUNK)
        band = (dc >= 0) & (dc <= WIN_CHUNKS)
        b = jnp.transpose(_lookup(table, _rel_bucket(kk - qq)), (2, 0, 1))
        out.append(jnp.where(band[None], b, NEG_INF))
    return jnp.stack(out)


def kernel(x, w_in, w_out, norm_mix, norm_ffn, kv_norm, w_uk, w_uv, w_gate2, b_gate,
           gla_norm, sinks, rel_bias, w_ffn_gate, w_ffn_up, w_ffn_down, final_norm):
    bsz, s_len, _ = x.shape
    t_tok = bsz * s_len
    wd = MXU_DTYPE
    biasn, biasf = _dsa_bias_tables(rel_bias)
    maskbias = _swa_maskbias(rel_bias)
    x2 = x.reshape(t_tok, D_MODEL)
    row = lambda v: v.reshape(1, -1).astype(F32)
    for l in range(DEPTH):
        w_arr = _arrange_w_in(w_in[l]).astype(wd)
        wuk = jnp.transpose(w_uk[l], (1, 2, 0)).astype(wd)
        wuv = jnp.transpose(w_uv[l], (1, 0, 2)).astype(wd)
        (qlat, ckv, ckvt, qidx, kidx, widxt, qb, kb, vb, glr, rb, qc, kc, vc) = _proj_call(
            x2, row(norm_mix[l]), w_arr, wuk, row(kv_norm[l]), bsz, s_len)
        b3 = lambda a: a.reshape(bsz, s_len, a.shape[-1])
        out_a = _dsa_call(qidx, widxt, qlat, b3(kidx), b3(ckv), ckvt,
                          biasn, biasf, wuv, bsz, s_len)
        wg2 = jnp.zeros((LANES, H_B * DK_B), F32).at[:GATE_RANK].set(w_gate2[l])
        out_b = _gla_call(b3(qb), b3(kb), b3(vb), b3(glr), b3(rb), wg2,
                          row(b_gate[l]), row(gla_norm[l]), bsz, s_len)
        out_c = _swa_call(sinks[l].astype(F32), b3(qc), b3(kc), b3(vc), maskbias, bsz, s_len)
        f2 = lambda a: a.reshape(t_tok, a.shape[-1])
        x2 = _ffn_call(x2, f2(out_a), f2(out_b), f2(out_c), w_out[l].astype(wd),
                       row(norm_ffn[l]), w_ffn_gate[l].astype(wd), w_ffn_up[l].astype(wd),
                       w_ffn_down[l].astype(wd), row(final_norm), l == DEPTH - 1)
    return x2.reshape(bsz, s_len, D_MODEL)
```

```python
import functools
import math

import jax
import jax.numpy as jnp
from jax import lax
from jax.experimental import pallas as pl
from jax.experimental.pallas import tpu as pltpu

D_MODEL = 1024
DEPTH = 2
CHUNK = 64
HEAD_DIM = 64
H_A = 4
KV_RANK = 128
IDX_HEADS = 4
IDX_DIM = 64
TOPK_MAX = 256
H_B = 4
DK_B = 64
DV_B = 128
GATE_RANK = 16
GATE_TEMP = 16.0
H_C = 4
KV_C = 2
WINDOW = 128
WIN_CHUNKS = WINDOW // CHUNK
NUM_BUCKETS = 32
MAX_DISTANCE = 128
D_FF = ((8 * D_MODEL // 3 + 255) // 256) * 256
EPS = 1e-6

IN_SPLITS = (H_A * HEAD_DIM, KV_RANK, IDX_HEADS * IDX_DIM, IDX_DIM, IDX_HEADS,
             H_B * DK_B, H_B * DK_B, H_B * DV_B, GATE_RANK, H_B * DV_B,
             H_C * HEAD_DIM, KV_C * HEAD_DIM, KV_C * HEAD_DIM)

LANES = 128
SUBLANES = 8
VMEM_LIMIT = 56 * 1024 * 1024

MXU_DTYPE = jnp.bfloat16
COARSE_DTYPE = jnp.bfloat16
F32 = jnp.float32
NEG_INF = float("-inf")
MASKED = -1e30
F32_MAX = float(jnp.finfo(jnp.float32).max)
INT_MIN = -2 ** 31
KEY_NEG_INF = INT_MIN + 2 ** 23
KEY_MIN_NORMAL = 2 ** 23
COARSE_STEP = 2 ** 16
FINE_STEPS = 18

QBLK = 256
KV_EXT = KV_RANK + 16
ROW_CHUNK = 32
TM_PROJ = 512
TS_GLA = 256
TQ_SWA = 256
TM_FFN = 512
TF_FFN = D_FF


def _dot(a, b):
    return jnp.dot(a, b, preferred_element_type=F32)


def _dot_nt(a, b):
    return lax.dot_general(a, b, (((1,), (1,)), ((), ())), preferred_element_type=F32)


def _dot_tn(a, b):
    return lax.dot_general(a, b, (((0,), (0,)), ((), ())), preferred_element_type=F32)


def _rms(x, g):
    return x * lax.rsqrt(jnp.mean(x * x, axis=-1, keepdims=True) + EPS) * g


_C_QA = 0
_C_CKV = 256
_C_QIDX = 384
_C_KW = 640
_C_QB = 768
_C_KB = 1024
_C_VB = 1280
_C_GLR = 1792
_C_RB = 1920
_C_QC = 2432
_C_KC = 2688
_C_VC = 2816
_NP = 2944


def _proj_kernel(x_ref, g_ref, w_ref, wuk_ref, kvn_ref,
                 qlat_ref, ckv_ref, ckvt_ref, qidx_ref, kidx_ref, widxt_ref,
                 qb_ref, kb_ref, vb_ref, glr_ref, rb_ref, qc_ref, kc_ref, vc_ref):
    h = _rms(x_ref[...], g_ref[...]).astype(MXU_DTYPE)

    def seg(c0, width):
        return _dot(h, w_ref[:, c0:c0 + width])

    qa = seg(_C_QA, 256)
    for hd in range(H_A):
        qh = qa[:, hd * HEAD_DIM:(hd + 1) * HEAD_DIM].astype(MXU_DTYPE)
        ql = _dot(qh, wuk_ref[hd]) * (HEAD_DIM ** -0.5)
        for t in range(TM_PROJ // QBLK):
            qlat_ref[0, t, :, hd * QBLK:(hd + 1) * QBLK] = (
                ql[t * QBLK:(t + 1) * QBLK, :].T.astype(qlat_ref.dtype))

    ckv = _rms(seg(_C_CKV, KV_RANK), kvn_ref[...])
    ckv_ref[...] = ckv.astype(ckv_ref.dtype)
    for t in range(TM_PROJ // QBLK):
        ckvt_ref[0, t, 0:KV_RANK, :] = ckv[t * QBLK:(t + 1) * QBLK, :].T.astype(ckvt_ref.dtype)
        ckvt_ref[0, t, KV_RANK:, :] = jnp.ones((KV_EXT - KV_RANK, QBLK), ckvt_ref.dtype)

    qi = seg(_C_QIDX, 256)
    for t in range(TM_PROJ // QBLK):
        qit = qi[t * QBLK:(t + 1) * QBLK, :].T
        for hd in range(IDX_HEADS):
            qidx_ref[0, t, :, hd * QBLK:(hd + 1) * QBLK] = (
                qit[hd * IDX_DIM:(hd + 1) * IDX_DIM, :].astype(qidx_ref.dtype))
    kw = seg(_C_KW, LANES)
    kidx_ref[...] = kw[:, :IDX_DIM].astype(kidx_ref.dtype)
    kwt = kw.T
    widxt_ref[0] = kwt[IDX_DIM:IDX_DIM + SUBLANES, :] * (IDX_HEADS ** -0.5 * IDX_DIM ** -0.5)

    qb_ref[...] = seg(_C_QB, 256).astype(qb_ref.dtype)
    kb_ref[...] = seg(_C_KB, 256).astype(kb_ref.dtype)
    vb_ref[...] = seg(_C_VB, 512).astype(vb_ref.dtype)
    glr_ref[...] = seg(_C_GLR, LANES)
    rb_ref[...] = seg(_C_RB, 512).astype(rb_ref.dtype)
    qc_ref[...] = seg(_C_QC, 256).astype(qc_ref.dtype)
    kc_ref[...] = seg(_C_KC, LANES).astype(kc_ref.dtype)
    vc_ref[...] = seg(_C_VC, LANES).astype(vc_ref.dtype)


def _arrange_w_in(w):
    offs = [0]
    for s in IN_SPLITS:
        offs.append(offs[-1] + s)
    parts = [w[:, offs[i]:offs[i + 1]] for i in range(len(IN_SPLITS))]
    (qa, ckv, qidx, kidx, widx, qb, kb, vb, glr, rb, qc, kc, vc) = parts
    z = lambda n: jnp.zeros((w.shape[0], n), w.dtype)
    return jnp.concatenate(
        [qa, ckv, qidx, kidx, widx, z(LANES - IDX_DIM - IDX_HEADS), qb, kb, vb,
         glr, z(LANES - GATE_RANK), rb, qc, kc, vc], axis=1)


def _proj_call(x2, g, w, wuk, kvn, bsz, s_len):
    t_tok = x2.shape[0]
    tm = TM_PROJ
    n_s = s_len // tm
    act = MXU_DTYPE

    def tok(width):
        return pl.BlockSpec((tm, width), lambda i: (i, 0))

    def const(shape):
        return pl.BlockSpec(shape, lambda i: (0,) * len(shape))

    def qblocks(depth):
        return pl.BlockSpec((1, tm // QBLK, depth, H_A * QBLK), lambda i: (i // n_s, i % n_s, 0, 0))

    out_shape = [
        jax.ShapeDtypeStruct((bsz, s_len // QBLK, KV_RANK, H_A * QBLK), act),
        jax.ShapeDtypeStruct((t_tok, KV_RANK), act),
        jax.ShapeDtypeStruct((bsz, s_len // QBLK, KV_EXT, QBLK), act),
        jax.ShapeDtypeStruct((bsz, s_len // QBLK, IDX_DIM, IDX_HEADS * QBLK), act),
        jax.ShapeDtypeStruct((t_tok, IDX_DIM), act),
        jax.ShapeDtypeStruct((bsz, SUBLANES, s_len), F32),
        jax.ShapeDtypeStruct((t_tok, H_B * DK_B), act),
        jax.ShapeDtypeStruct((t_tok, H_B * DK_B), act),
        jax.ShapeDtypeStruct((t_tok, H_B * DV_B), act),
        jax.ShapeDtypeStruct((t_tok, LANES), F32),
        jax.ShapeDtypeStruct((t_tok, H_B * DV_B), act),
        jax.ShapeDtypeStruct((t_tok, H_C * HEAD_DIM), act),
        jax.ShapeDtypeStruct((t_tok, KV_C * HEAD_DIM), act),
        jax.ShapeDtypeStruct((t_tok, KV_C * HEAD_DIM), act),
    ]
    out_specs = [
        qblocks(KV_RANK), tok(KV_RANK),
        pl.BlockSpec((1, tm // QBLK, KV_EXT, QBLK), lambda i: (i // n_s, i % n_s, 0, 0)),
        qblocks(IDX_DIM), tok(IDX_DIM),
        pl.BlockSpec((1, SUBLANES, tm), lambda i: (i // n_s, 0, i % n_s)),
        tok(H_B * DK_B), tok(H_B * DK_B), tok(H_B * DV_B), tok(LANES), tok(H_B * DV_B),
        tok(H_C * HEAD_DIM), tok(KV_C * HEAD_DIM), tok(KV_C * HEAD_DIM),
    ]
    return pl.pallas_call(
        _proj_kernel,
        grid=(t_tok // tm,),
        in_specs=[tok(D_MODEL), const((1, D_MODEL)), const((D_MODEL, _NP)),
                  const((H_A, HEAD_DIM, KV_RANK)), const((1, KV_RANK))],
        out_specs=out_specs,
        out_shape=out_shape,
        compiler_params=pltpu.CompilerParams(
            dimension_semantics=("parallel",), vmem_limit_bytes=VMEM_LIMIT),
        name="norm_in_proj",
    )(x2, g, w, wuk, kvn)


def _key_to_float(k):
    k = jnp.where((k > 0) & (k < KEY_MIN_NORMAL), KEY_MIN_NORMAL, k)
    k = jnp.where((k < 0) & (k > -KEY_MIN_NORMAL), 0, k)
    bits = jnp.where(k >= 0, k, INT_MIN - k)
    return lax.bitcast_convert_type(bits, F32)


def _dsa_kernel(qidx_ref, widxt_ref, qlat_ref, kidx_ref, ckv_ref, ckvt_ref,
                biasn_ref, wuv_ref, out_ref,
                s_ref, s16_ref, lg_ref, acc_ref, da_ref, db_ref, pa_ref, pb_ref, *, s_len, topk):
    jq = pl.program_id(1)
    n_tiles = jq + 1
    n_virtual = s_len - n_tiles * QBLK
    hq = H_A * QBLK
    head_cols = [slice(hd * QBLK, (hd + 1) * QBLK) for hd in range(H_A)]

    def rows(j, r0=0, n=QBLK):
        return pl.ds(pl.multiple_of(j * QBLK + r0, n), n)

    def fold(x, op):
        return op(x.reshape(x.shape[0] // SUBLANES, SUBLANES, x.shape[-1]), axis=0)

    def pipeline(n, produce, consume, carry, bufs):
        a, b = bufs
        produce(jnp.maximum(n - 1, 0), a)

        def pair(i, carry):
            j = n - 1 - 2 * i
            produce(j - 1, b)
            carry = consume(j, a, carry)
            produce(jnp.maximum(j - 2, 0), a)
            return consume(j - 1, b, carry)

        carry = lax.fori_loop(0, n // 2, pair, carry)
        return lax.cond(n % 2 == 1, lambda c: consume(0, a, c), lambda c: c, carry)

    qis = qidx_ref[0, 0]
    wt = widxt_ref[0]

    def stage_scores(j, buf):
        buf[...] = _dot(kidx_ref[0, rows(j), :], qis)

    def put_scores(j, buf, carry, diagonal=False):
        for r0 in range(0, QBLK, ROW_CHUNK):
            rc = slice(r0, r0 + ROW_CHUNK)
            s = jnp.maximum(buf[rc, head_cols[0]], 0.0) * wt[0:1, :]
            for hd in range(1, IDX_HEADS):
                s = s + jnp.maximum(buf[rc, head_cols[hd]], 0.0) * wt[hd:hd + 1, :]
            if diagonal:
                kchunk = (r0 + lax.broadcasted_iota(jnp.int32, (ROW_CHUNK, 1), 0)) // CHUNK
                qchunk = lax.broadcasted_iota(jnp.int32, (1, QBLK), 1) // CHUNK
                s = jnp.where(kchunk <= qchunk, s, NEG_INF)
            s_ref[rows(j, r0, ROW_CHUNK), :] = s
            s16_ref[rows(j, r0, ROW_CHUNK), :] = s.astype(s16_ref.dtype)
        return carry

    stage_scores(jq, da_ref)
    put_scores(jq, da_ref, 0, diagonal=True)
    pipeline(jq, stage_scores, put_scores, 0, (da_ref, db_ref))

    def count_tiles(tile_count, zero):
        acc = lax.fori_loop(0, n_tiles // 2,
                            lambda i, acc: acc + (tile_count(2 * i) + tile_count(2 * i + 1)), zero)
        return lax.cond(n_tiles % 2 == 1, lambda acc: acc + tile_count(n_tiles - 1),
                        lambda acc: acc, acc)

    def virtual_hits(tf):
        return jnp.where(NEG_INF >= tf, n_virtual, 0)

    def count_ge_rounded(tf):
        tb = tf.astype(COARSE_DTYPE)
        one, zero = jnp.ones((), COARSE_DTYPE), jnp.zeros((), COARSE_DTYPE)
        packed = 2 * SUBLANES

        def tile_count(j):
            hit = jnp.where(s16_ref[rows(j), :] >= tb, one, zero)
            slabs = [hit[r0:r0 + packed, :] for r0 in range(0, QBLK, packed)]
            while len(slabs) > 1:
                slabs = [a + b for a, b in zip(slabs[0::2], slabs[1::2])]
            return slabs[0]

        acc = count_tiles(tile_count, jnp.zeros((packed, QBLK), COARSE_DTYPE))
        total = jnp.sum(acc.astype(F32), axis=0, keepdims=True).astype(jnp.int32)
        return total + virtual_hits(tf)

    def count_ge(tf):
        acc = count_tiles(lambda j: fold(jnp.where(s_ref[rows(j), :] >= tf, 1, 0), jnp.sum),
                          jnp.zeros((SUBLANES, QBLK), jnp.int32))
        return jnp.sum(acc, axis=0, keepdims=True) + virtual_hits(tf)

    def coarse_step(i, t_key):
        cand = t_key + jnp.left_shift(jnp.int32(1), 31 - i)
        return jnp.where(count_ge_rounded(_key_to_float(cand)) >= topk, cand, t_key)

    g_key = lax.fori_loop(0, 16, coarse_step, jnp.full((1, QBLK), INT_MIN, jnp.int32))

    def fine_step(i, carry):
        lo, hi, cnt_lo = carry
        mid = lo + jnp.right_shift(hi - lo, 1)
        cnt = count_ge(_key_to_float(mid))
        ok = cnt >= topk
        return jnp.where(ok, mid, lo), jnp.where(ok, hi, mid), jnp.where(ok, cnt, cnt_lo)

    lo0 = jnp.maximum(g_key - COARSE_STEP, KEY_NEG_INF)
    t_key, _, cnt_t = lax.fori_loop(
        0, FINE_STEPS, fine_step,
        (lo0, g_key + (COARSE_STEP + 1), jnp.full((1, QBLK), -1, jnp.int32)))
    tf = _key_to_float(t_key)

    has_tie = jnp.max(jnp.where((cnt_t == topk) | (tf == NEG_INF), 0, 1))

    @pl.when(has_tie > 0)
    def _():
        t_next = _key_to_float(t_key + 1)
        need = (topk - count_ge(t_next)).astype(F32)
        r = lax.broadcasted_iota(jnp.int32, (QBLK, QBLK), 0)
        c = lax.broadcasted_iota(jnp.int32, (QBLK, QBLK), 1)
        tril = (c <= r).astype(MXU_DTYPE)

        def body(j, carry):
            tile = s_ref[rows(j), :]
            eq = (tile >= tf) & jnp.logical_not(tile >= t_next)
            pre = _dot(tril, eq.astype(MXU_DTYPE)) + carry
            s_ref[rows(j), :] = jnp.where(eq & (pre > need), NEG_INF, tile)
            return pre[QBLK - 1:QBLK, :]

        lax.fori_loop(0, n_tiles, body, jnp.zeros((1, QBLK), F32))

    t_sel = jnp.maximum(tf, -F32_MAX)
    qls = qlat_ref[0, 0]

    def stage_logits(j, buf):
        buf[...] = _dot(ckv_ref[0, rows(j), :], qls)

    def mask_logits(j, buf, mx, near=None):
        parts = [mx[:, head_cols[hd]] for hd in range(H_A)]
        for r0 in range(0, QBLK, ROW_CHUNK):
            rc = slice(r0, r0 + ROW_CHUNK)
            sel = s_ref[rows(j, r0, ROW_CHUNK), :] >= t_sel
            for hd in range(H_A):
                v = buf[rc, head_cols[hd]]
                if near is not None:
                    v = v + biasn_ref[near, rc, head_cols[hd]]
                v = jnp.where(sel, v, MASKED)
                lg_ref[rows(j, r0, ROW_CHUNK), head_cols[hd]] = v
                parts[hd] = jnp.maximum(parts[hd], fold(v, jnp.max))
        return jnp.concatenate(parts, axis=-1)

    stage_logits(jq, da_ref)
    mx = mask_logits(jq, da_ref, jnp.full((SUBLANES, hq), MASKED, F32), near=0)

    def previous_tile(mx):
        stage_logits(jq - 1, db_ref)
        return mask_logits(jq - 1, db_ref, mx, near=1)

    mx = lax.cond(jq >= 1, previous_tile, lambda mx: mx, mx)
    mx = pipeline(jnp.maximum(jq - 1, 0), stage_logits, mask_logits, mx, (da_ref, db_ref))
    m = jnp.max(mx, axis=0, keepdims=True)

    acc_ref[...] = jnp.zeros(acc_ref.shape, F32)

    def stage_probs(j, buf):
        for r0 in range(0, QBLK, ROW_CHUNK):
            p = jnp.exp(lg_ref[rows(j, r0, ROW_CHUNK), :] - m)
            buf[r0:r0 + ROW_CHUNK, :] = p.astype(buf.dtype)

    def accumulate(j, buf, carry):
        acc_ref[...] += _dot(ckvt_ref[0, j], buf[...])
        return carry

    pipeline(n_tiles, stage_probs, accumulate, 0, (pa_ref, pb_ref))

    l = acc_ref[KV_RANK:KV_RANK + 1, :]
    outs = []
    for hd in range(H_A):
        o_t = acc_ref[0:KV_RANK, head_cols[hd]] / l[:, head_cols[hd]]
        outs.append(_dot(o_t.T.astype(MXU_DTYPE), wuv_ref[hd]))
    out_ref[0] = jnp.concatenate(outs, axis=-1).astype(out_ref.dtype)


def _dsa_call(qidx, widxt, qlat, kidx, ckv, ckvt, biasn, wuv, bsz, s_len):
    topk = min(TOPK_MAX, s_len // 4)
    nblk = s_len // QBLK
    qblock_t = lambda depth: pl.BlockSpec((1, 1, depth, H_A * QBLK), lambda b, i: (b, i, 0, 0))
    const = lambda shape: pl.BlockSpec(shape, lambda b, i: (0,) * len(shape))
    return pl.pallas_call(
        functools.partial(_dsa_kernel, s_len=s_len, topk=topk),
        grid=(bsz, nblk),
        in_specs=[
            qblock_t(IDX_DIM),
            pl.BlockSpec((1, SUBLANES, QBLK), lambda b, i: (b, 0, i)),
            qblock_t(KV_RANK),
            pl.BlockSpec((1, s_len, IDX_DIM), lambda b, i: (b, 0, 0)),
            pl.BlockSpec((1, s_len, KV_RANK), lambda b, i: (b, 0, 0)),
            pl.BlockSpec((1, nblk, KV_EXT, QBLK), lambda b, i: (b, 0, 0, 0)),
            const((2, QBLK, H_A * QBLK)),
            const((H_A, KV_RANK, HEAD_DIM)),
        ],
        out_specs=pl.BlockSpec((1, QBLK, H_A * HEAD_DIM), lambda b, i: (b, i, 0)),
        out_shape=jax.ShapeDtypeStruct((bsz, s_len, H_A * HEAD_DIM), MXU_DTYPE),
        scratch_shapes=[
            pltpu.VMEM((s_len, QBLK), F32),
            pltpu.VMEM((s_len, QBLK), COARSE_DTYPE),
            pltpu.VMEM((s_len, H_A * QBLK), F32),
            pltpu.VMEM((KV_EXT, H_A * QBLK), F32),
            pltpu.VMEM((QBLK, H_A * QBLK), F32),
            pltpu.VMEM((QBLK, H_A * QBLK), F32),
            pltpu.VMEM((QBLK, H_A * QBLK), MXU_DTYPE),
            pltpu.VMEM((QBLK, H_A * QBLK), MXU_DTYPE),
        ],
        compiler_params=pltpu.CompilerParams(
            dimension_semantics=("parallel", "arbitrary"), vmem_limit_bytes=VMEM_LIMIT),
        name="dsa_mixer",
    )(qidx, widxt, qlat, kidx, ckv, ckvt, biasn, wuv)


def _split3(x):
    a = x.astype(MXU_DTYPE)
    r = x - a.astype(F32)
    b = r.astype(MXU_DTYPE)
    c = (r - b.astype(F32)).astype(MXU_DTYPE)
    return a, b, c


def _gla_kernel(qb_ref, kb_ref, vb_ref, glr_ref, rb_ref, wg_ref, bg_ref, gn_ref,
                out_ref, st_ref):
    @pl.when(pl.program_id(1) == 0)
    def _():
        st_ref[...] = jnp.zeros(st_ref.shape, F32)

    g_hi, g_lo, _ = _split3(glr_ref[0])
    w_hi, w_lo, _ = _split3(wg_ref[...])
    z = _dot(g_hi, w_hi) + (_dot(g_hi, w_lo) + _dot(g_lo, w_hi)) + bg_ref[...]
    g = (jnp.minimum(z, 0.0) - jnp.log1p(jnp.exp(-jnp.abs(z)))) / GATE_TEMP

    r = lax.broadcasted_iota(jnp.int32, (CHUNK, CHUNK), 0)
    c = lax.broadcasted_iota(jnp.int32, (CHUNK, CHUNK), 1)
    causal = c <= r
    tril = causal.astype(MXU_DTYPE)

    state = [st_ref[hd] for hd in range(H_B)]
    for ci in range(TS_GLA // CHUNK):
        rows = slice(ci * CHUNK, (ci + 1) * CHUNK)
        g1, g2, g3 = _split3(g[rows, :])
        b = _dot(tril, g1) + (_dot(tril, g2) + _dot(tril, g3))
        b_last = b[CHUNK - 1:CHUNK, :]
        q = qb_ref[0, rows, :].astype(F32) * (DK_B ** -0.5)
        k = kb_ref[0, rows, :].astype(F32)
        qe = (q * jnp.exp(b)).astype(MXU_DTYPE)
        ke = (k * jnp.exp(-b)).astype(MXU_DTYPE)
        kd = (k * jnp.exp(b_last - b)).astype(MXU_DTYPE)
        decay = jnp.exp(b_last)
        for hd in range(H_B):
            ks = slice(hd * DK_B, (hd + 1) * DK_B)
            vs = slice(hd * DV_B, (hd + 1) * DV_B)
            v = vb_ref[0, rows, vs]
            att = jnp.where(causal, _dot_nt(qe[:, ks], ke[:, ks]), 0.0)
            st = state[hd]
            o = _dot(att.astype(MXU_DTYPE), v) + _dot_nt(qe[:, ks], st.astype(MXU_DTYPE))
            state[hd] = st * decay[:, ks] + _dot_tn(v, kd[:, ks])
            o = o * lax.rsqrt(jnp.mean(o * o, axis=-1, keepdims=True) + EPS) * gn_ref[:, vs]
            rg = rb_ref[0, rows, vs].astype(F32)
            o = o * (rg / (1.0 + jnp.exp(-rg)))
            out_ref[0, rows, vs] = o.astype(out_ref.dtype)
    for hd in range(H_B):
        st_ref[hd] = state[hd]


def _gla_call(qb, kb, vb, glr, rb, wg, bg, gn, bsz, s_len):
    ts = TS_GLA
    tok = lambda width: pl.BlockSpec((1, ts, width), lambda b, i: (b, i, 0))
    const = lambda shape: pl.BlockSpec(shape, lambda b, i: (0,) * len(shape))
    return pl.pallas_call(
        _gla_kernel,
        grid=(bsz, s_len // ts),
        in_specs=[tok(H_B * DK_B), tok(H_B * DK_B), tok(H_B * DV_B), tok(LANES), tok(H_B * DV_B),
                  const((LANES, H_B * DK_B)), const((1, H_B * DK_B)), const((1, H_B * DV_B))],
        out_specs=tok(H_B * DV_B),
        out_shape=jax.ShapeDtypeStruct((bsz, s_len, H_B * DV_B), MXU_DTYPE),
        scratch_shapes=[pltpu.VMEM((H_B, DV_B, DK_B), F32)],
        compiler_params=pltpu.CompilerParams(
            dimension_semantics=("parallel", "arbitrary"), vmem_limit_bytes=VMEM_LIMIT),
        name="gla_mixer",
    )(qb, kb, vb, glr, rb, wg, bg, gn)


SWA_SUB = WINDOW
SWA_WIN = SWA_SUB + WINDOW


def _swa_window_start(tile_start, sub):
    return jnp.maximum(tile_start + sub * SWA_SUB - WINDOW, 0)


def _swa_kernel(sink_ref, qc_ref, kc_ref, vc_ref, mb_ref, out_ref):
    i = pl.program_id(1)
    grp = H_C // KV_C
    for sub in range(TQ_SWA // SWA_SUB):
        qrows = slice(sub * SWA_SUB, (sub + 1) * SWA_SUB)
        start = pl.multiple_of(_swa_window_start(i * TQ_SWA, sub), WINDOW)
        k = kc_ref[0, pl.ds(start, SWA_WIN), :]
        v = vc_ref[0, pl.ds(start, SWA_WIN), :]
        outs = []
        for hd in range(H_C):
            kv = hd // grp
            q = qc_ref[0, qrows, hd * HEAD_DIM:(hd + 1) * HEAD_DIM]
            lg = (_dot_nt(q, k[:, kv * HEAD_DIM:(kv + 1) * HEAD_DIM]) * (HEAD_DIM ** -0.5)
                  + mb_ref[0, hd, sub])
            sink = sink_ref[hd]
            m = jnp.maximum(jnp.max(lg, axis=-1, keepdims=True), sink)
            e = jnp.exp(lg - m)
            p = e / (jnp.sum(e, axis=-1, keepdims=True) + jnp.exp(sink - m))
            outs.append(_dot(p.astype(MXU_DTYPE), v[:, kv * HEAD_DIM:(kv + 1) * HEAD_DIM]))
        out_ref[0, qrows, :] = jnp.concatenate(outs, axis=-1).astype(out_ref.dtype)


def _swa_call(sinks, qc, kc, vc, maskbias, bsz, s_len):
    tq = TQ_SWA
    return pl.pallas_call(
        _swa_kernel,
        grid=(bsz, s_len // tq),
        in_specs=[
            pl.BlockSpec(memory_space=pltpu.SMEM),
            pl.BlockSpec((1, tq, H_C * HEAD_DIM), lambda b, i: (b, i, 0)),
            pl.BlockSpec((1, s_len, KV_C * HEAD_DIM), lambda b, i: (b, 0, 0)),
            pl.BlockSpec((1, s_len, KV_C * HEAD_DIM), lambda b, i: (b, 0, 0)),
            pl.BlockSpec((1, H_C, tq // SWA_SUB, SWA_SUB, SWA_WIN),
                         lambda b, i: (jnp.minimum(i, 1), 0, 0, 0, 0)),
        ],
        out_specs=pl.BlockSpec((1, tq, H_C * HEAD_DIM), lambda b, i: (b, i, 0)),
        out_shape=jax.ShapeDtypeStruct((bsz, s_len, H_C * HEAD_DIM), MXU_DTYPE),
        compiler_params=pltpu.CompilerParams(
            dimension_semantics=("parallel", "arbitrary"), vmem_limit_bytes=VMEM_LIMIT),
        name="swa_mixer",
    )(sinks, qc, kc, vc, maskbias)


def _ffn_kernel(x_ref, oa_ref, ob_ref, oc_ref, gf_ref, gfin_ref,
                wo_hbm, wg_hbm, wu_hbm, wd_hbm, out_ref,
                wo_ref, wg_ref, wu_ref, wd_ref, sems, *, final_norm):
    @pl.when(pl.program_id(0) == 0)
    def _():
        pairs = ((wo_hbm, wo_ref), (wg_hbm, wg_ref), (wu_hbm, wu_ref), (wd_hbm, wd_ref))
        copies = [pltpu.make_async_copy(src, dst, sems.at[k]) for k, (src, dst) in enumerate(pairs)]
        for cp in copies:
            cp.start()
        for cp in copies:
            cp.wait()

    na = H_A * HEAD_DIM
    nb = H_B * DV_B
    x1 = (x_ref[...] + _dot(oa_ref[...], wo_ref[0:na, :])
          + _dot(ob_ref[...], wo_ref[na:na + nb, :])
          + _dot(oc_ref[...], wo_ref[na + nb:, :]))
    h = _rms(x1, gf_ref[...]).astype(MXU_DTYPE)
    y = x1
    for j in range(D_FF // TF_FFN):
        cols = slice(j * TF_FFN, (j + 1) * TF_FFN)
        gate = _dot(h, wg_ref[:, cols])
        up = _dot(h, wu_ref[:, cols])
        a = (gate / (1.0 + jnp.exp(-gate))) * up
        y = y + _dot(a.astype(MXU_DTYPE), wd_ref[cols, :])
    if final_norm:
        y = _rms(y, gfin_ref[...])
    out_ref[...] = y


def _ffn_call(x2, oa, ob, oc, wo, gf, wg, wu, wd, gfin, final_norm):
    t_tok = x2.shape[0]
    tm = TM_FFN
    tok = lambda width: pl.BlockSpec((tm, width), lambda i: (i, 0))
    const = lambda shape: pl.BlockSpec(shape, lambda i: (0,) * len(shape))
    in_hbm = pl.BlockSpec(memory_space=pl.ANY)
    return pl.pallas_call(
        functools.partial(_ffn_kernel, final_norm=final_norm),
        grid=(t_tok // tm,),
        in_specs=[tok(D_MODEL), tok(H_A * HEAD_DIM), tok(H_B * DV_B), tok(H_C * HEAD_DIM),
                  const((1, D_MODEL)), const((1, D_MODEL)), in_hbm, in_hbm, in_hbm, in_hbm],
        out_specs=tok(D_MODEL),
        out_shape=jax.ShapeDtypeStruct((t_tok, D_MODEL), F32),
        scratch_shapes=[pltpu.VMEM((D_MODEL, D_MODEL), MXU_DTYPE),
                        pltpu.VMEM((D_MODEL, D_FF), MXU_DTYPE),
                        pltpu.VMEM((D_MODEL, D_FF), MXU_DTYPE),
                        pltpu.VMEM((D_FF, D_MODEL), MXU_DTYPE),
                        pltpu.SemaphoreType.DMA((4,))],
        compiler_params=pltpu.CompilerParams(
            dimension_semantics=("arbitrary",), vmem_limit_bytes=VMEM_LIMIT),
        name="out_proj_ffn",
    )(x2, oa, ob, oc, gf, gfin, wo, wg, wu, wd)


def _rel_bucket(rel):
    nb = NUM_BUCKETS // 2
    max_exact = nb // 2
    n = jnp.abs(rel)
    side = jnp.where(rel > 0, nb, 0)
    nf = jnp.maximum(n, 1).astype(jnp.float32)
    large = max_exact + (jnp.log(nf / max_exact) / math.log(MAX_DISTANCE / max_exact)
                         * (nb - max_exact)).astype(jnp.int32)
    large = jnp.minimum(large, nb - 1)
    return side + jnp.where(n < max_exact, n, large)


def _bucket_ids(ndim, axis):
    shape = [1] * ndim
    shape[axis] = NUM_BUCKETS
    return jnp.arange(NUM_BUCKETS, dtype=jnp.int32).reshape(shape)


def _dsa_bias_tables(rel_bias):
    table = rel_bias[:, :H_A].astype(F32)
    far = _rel_bucket(jnp.full((), -2 * QBLK, jnp.int32))
    table = table - jnp.sum(jnp.where(_bucket_ids(2, 0) == far, table, 0.0), axis=0, keepdims=True)
    kk = jnp.arange(QBLK, dtype=jnp.int32)[:, None]
    qq = jnp.arange(QBLK, dtype=jnp.int32)[None, :]
    near = []
    for shift in (0, QBLK):
        bucket = _rel_bucket(kk - shift - qq)[:, None, None, :]
        b = jnp.sum(jnp.where(bucket == _bucket_ids(4, 1), table[None, :, :, None], 0.0), axis=1)
        near.append(b.reshape(QBLK, H_A * QBLK))
    return jnp.stack(near)


def _swa_maskbias(rel_bias):
    table = rel_bias[:, H_A:].astype(F32)
    out = []
    for tile_start in (0, TQ_SWA):
        subs = []
        for sub in range(TQ_SWA // SWA_SUB):
            qq = tile_start + sub * SWA_SUB + jnp.arange(SWA_SUB, dtype=jnp.int32)[:, None]
            kk = (max(tile_start + sub * SWA_SUB - WINDOW, 0)
                  + jnp.arange(SWA_WIN, dtype=jnp.int32)[None, :])
            dc = qq // CHUNK - kk // CHUNK
            band = (dc >= 0) & (dc <= WIN_CHUNKS)
            bucket = _rel_bucket(kk - qq)[None, None]
            b = jnp.sum(jnp.where(bucket == _bucket_ids(4, 0), table[:, :, None, None], 0.0), axis=0)
            subs.append(jnp.where(band[None], b, NEG_INF))
        out.append(jnp.stack(subs, axis=1))
    return jnp.stack(out)


def kernel(x, w_in, w_out, norm_mix, norm_ffn, kv_norm, w_uk, w_uv, w_gate2, b_gate,
           gla_norm, sinks, rel_bias, w_ffn_gate, w_ffn_up, w_ffn_down, final_norm):
    bsz, s_len, _ = x.shape
    t_tok = bsz * s_len
    wd = MXU_DTYPE
    biasn = _dsa_bias_tables(rel_bias)
    maskbias = _swa_maskbias(rel_bias)
    x2 = x.reshape(t_tok, D_MODEL)
    row = lambda v: v.reshape(1, -1).astype(F32)
    for l in range(DEPTH):
        w_arr = _arrange_w_in(w_in[l]).astype(wd)
        wuk = jnp.transpose(w_uk[l], (1, 2, 0)).astype(wd)
        wuv = jnp.transpose(w_uv[l], (1, 0, 2)).astype(wd)
        (qlat, ckv, ckvt, qidx, kidx, widxt, qb, kb, vb, glr, rb, qc, kc, vc) = _proj_call(
            x2, row(norm_mix[l]), w_arr, wuk, row(kv_norm[l]), bsz, s_len)
        b3 = lambda a: a.reshape(bsz, s_len, a.shape[-1])
        out_a = _dsa_call(qidx, widxt, qlat, b3(kidx), b3(ckv), ckvt,
                          biasn, wuv, bsz, s_len)
        wg2 = jnp.zeros((LANES, H_B * DK_B), F32).at[:GATE_RANK].set(w_gate2[l])
        out_b = _gla_call(b3(qb), b3(kb), b3(vb), b3(glr), b3(rb), wg2,
                          row(b_gate[l]), row(gla_norm[l]), bsz, s_len)
        out_c = _swa_call(sinks[l].astype(F32), b3(qc), b3(kc), b3(vc), maskbias, bsz, s_len)
        f2 = lambda a: a.reshape(t_tok, a.shape[-1])
        x2 = _ffn_call(x2, f2(out_a), f2(out_b), f2(out_c), w_out[l].astype(wd),
                       row(norm_ffn[l]), w_ffn_gate[l].astype(wd), w_ffn_up[l].astype(wd),
                       w_ffn_down[l].astype(wd), row(final_norm), l == DEPTH - 1)
    return x2.reshape(bsz, s_len, D_MODEL)
```

```python
import functools
import math

import jax
import jax.numpy as jnp
from jax import lax
from jax.experimental import pallas as pl
from jax.experimental.pallas import tpu as pltpu

D_MODEL = 1024
DEPTH = 2
CHUNK = 64
HEAD_DIM = 64
H_A = 4
KV_RANK = 128
IDX_HEADS = 4
IDX_DIM = 64
TOPK_MAX = 256
H_B = 4
DK_B = 64
DV_B = 128
GATE_RANK = 16
GATE_TEMP = 16.0
H_C = 4
KV_C = 2
WINDOW = 128
WIN_CHUNKS = WINDOW // CHUNK
NUM_BUCKETS = 32
MAX_DISTANCE = 128
D_FF = ((8 * D_MODEL // 3 + 255) // 256) * 256
EPS = 1e-6

IN_SPLITS = (H_A * HEAD_DIM, KV_RANK, IDX_HEADS * IDX_DIM, IDX_DIM, IDX_HEADS,
             H_B * DK_B, H_B * DK_B, H_B * DV_B, GATE_RANK, H_B * DV_B,
             H_C * HEAD_DIM, KV_C * HEAD_DIM, KV_C * HEAD_DIM)

LANES = 128
SUBLANES = 8
VMEM_LIMIT = 56 * 1024 * 1024

MXU_DTYPE = jnp.bfloat16
COARSE_DTYPE = jnp.bfloat16
F32 = jnp.float32
NEG_INF = float("-inf")
MASKED = -1e30
F32_MAX = float(jnp.finfo(jnp.float32).max)
INT_MIN = -2 ** 31
KEY_NEG_INF = INT_MIN + 2 ** 23
KEY_MIN_NORMAL = 2 ** 23
COARSE_STEP = 2 ** 16
FINE_STEPS = 18

QBLK = 256
KV_EXT = KV_RANK + 16
ROW_CHUNK = 32
TM_PROJ = 512
TS_GLA = 256
TQ_SWA = 256
TM_FFN = 512
TF_FFN = D_FF


def _dot(a, b):
    return jnp.dot(a, b, preferred_element_type=F32)


def _dot_nt(a, b):
    return lax.dot_general(a, b, (((1,), (1,)), ((), ())), preferred_element_type=F32)


def _dot_tn(a, b):
    return lax.dot_general(a, b, (((0,), (0,)), ((), ())), preferred_element_type=F32)


def _rms(x, g):
    return x * lax.rsqrt(jnp.mean(x * x, axis=-1, keepdims=True) + EPS) * g


_W_STARTS = (0, sum(IN_SPLITS[:5]), sum(IN_SPLITS[:9]))
_W_WIDTHS = (768, 1152, 1024)
_C_QA = (0, 0)
_C_CKV = (0, 256)
_C_QIDX = (0, 384)
_C_KW = (0, 640)
_C_QB = (1, 0)
_C_KB = (1, 256)
_C_VB = (1, 512)
_C_GLR = (1, 1024)
_C_RB = (2, 0)
_C_QC = (2, 512)
_C_KC = (2, 768)
_C_VC = (2, 896)


def _proj_kernel(x_ref, g_ref, wa_ref, wb_ref, wc_ref, wuk_ref, kvn_ref,
                 qlat_ref, ckv_ref, ckvt_ref, qidx_ref, kidx_ref, widxt_ref,
                 qb_ref, kb_ref, vb_ref, glr_ref, rb_ref, qc_ref, kc_ref, vc_ref):
    h = _rms(x_ref[...], g_ref[...]).astype(MXU_DTYPE)

    def seg(where, width):
        window, c0 = where
        return _dot(h, (wa_ref, wb_ref, wc_ref)[window][:, c0:c0 + width])

    qa = seg(_C_QA, 256)
    for hd in range(H_A):
        qh = qa[:, hd * HEAD_DIM:(hd + 1) * HEAD_DIM].astype(MXU_DTYPE)
        ql = _dot(qh, wuk_ref[hd]) * (HEAD_DIM ** -0.5)
        for t in range(TM_PROJ // QBLK):
            qlat_ref[0, t, :, hd * QBLK:(hd + 1) * QBLK] = (
                ql[t * QBLK:(t + 1) * QBLK, :].T.astype(qlat_ref.dtype))

    ckv = _rms(seg(_C_CKV, KV_RANK), kvn_ref[...])
    ckv_ref[...] = ckv.astype(ckv_ref.dtype)
    for t in range(TM_PROJ // QBLK):
        ckvt_ref[0, t, 0:KV_RANK, :] = ckv[t * QBLK:(t + 1) * QBLK, :].T.astype(ckvt_ref.dtype)
        ckvt_ref[0, t, KV_RANK:, :] = jnp.ones((KV_EXT - KV_RANK, QBLK), ckvt_ref.dtype)

    qi = seg(_C_QIDX, 256)
    for t in range(TM_PROJ // QBLK):
        qit = qi[t * QBLK:(t + 1) * QBLK, :].T
        for hd in range(IDX_HEADS):
            qidx_ref[0, t, :, hd * QBLK:(hd + 1) * QBLK] = (
                qit[hd * IDX_DIM:(hd + 1) * IDX_DIM, :].astype(qidx_ref.dtype))
    kw = seg(_C_KW, LANES)
    kidx_ref[...] = kw[:, :IDX_DIM].astype(kidx_ref.dtype)
    kwt = kw.T
    widxt_ref[0] = kwt[IDX_DIM:IDX_DIM + SUBLANES, :] * (IDX_HEADS ** -0.5 * IDX_DIM ** -0.5)

    qb_ref[...] = seg(_C_QB, 256).astype(qb_ref.dtype)
    kb_ref[...] = seg(_C_KB, 256).astype(kb_ref.dtype)
    vb_ref[...] = seg(_C_VB, 512).astype(vb_ref.dtype)
    glr_ref[...] = seg(_C_GLR, LANES)
    rb_ref[...] = seg(_C_RB, 512).astype(rb_ref.dtype)
    qc_ref[...] = seg(_C_QC, 256).astype(qc_ref.dtype)
    kc_ref[...] = seg(_C_KC, LANES).astype(kc_ref.dtype)
    vc_ref[...] = seg(_C_VC, LANES).astype(vc_ref.dtype)


def _split_w_in(w):
    return tuple(w[:, c0:c0 + n] for c0, n in zip(_W_STARTS, _W_WIDTHS))


def _proj_call(x2, g, w_windows, wuk, kvn, bsz, s_len):
    t_tok = x2.shape[0]
    tm = TM_PROJ
    n_s = s_len // tm
    act = MXU_DTYPE

    def tok(width):
        return pl.BlockSpec((tm, width), lambda i: (i, 0))

    def const(shape):
        return pl.BlockSpec(shape, lambda i: (0,) * len(shape))

    def qblocks(depth):
        return pl.BlockSpec((1, tm // QBLK, depth, H_A * QBLK), lambda i: (i // n_s, i % n_s, 0, 0))

    out_shape = [
        jax.ShapeDtypeStruct((bsz, s_len // QBLK, KV_RANK, H_A * QBLK), act),
        jax.ShapeDtypeStruct((t_tok, KV_RANK), act),
        jax.ShapeDtypeStruct((bsz, s_len // QBLK, KV_EXT, QBLK), act),
        jax.ShapeDtypeStruct((bsz, s_len // QBLK, IDX_DIM, IDX_HEADS * QBLK), act),
        jax.ShapeDtypeStruct((t_tok, IDX_DIM), act),
        jax.ShapeDtypeStruct((bsz, SUBLANES, s_len), F32),
        jax.ShapeDtypeStruct((t_tok, H_B * DK_B), act),
        jax.ShapeDtypeStruct((t_tok, H_B * DK_B), act),
        jax.ShapeDtypeStruct((t_tok, H_B * DV_B), act),
        jax.ShapeDtypeStruct((t_tok, LANES), F32),
        jax.ShapeDtypeStruct((t_tok, H_B * DV_B), act),
        jax.ShapeDtypeStruct((t_tok, H_C * HEAD_DIM), act),
        jax.ShapeDtypeStruct((t_tok, KV_C * HEAD_DIM), act),
        jax.ShapeDtypeStruct((t_tok, KV_C * HEAD_DIM), act),
    ]
    out_specs = [
        qblocks(KV_RANK), tok(KV_RANK),
        pl.BlockSpec((1, tm // QBLK, KV_EXT, QBLK), lambda i: (i // n_s, i % n_s, 0, 0)),
        qblocks(IDX_DIM), tok(IDX_DIM),
        pl.BlockSpec((1, SUBLANES, tm), lambda i: (i // n_s, 0, i % n_s)),
        tok(H_B * DK_B), tok(H_B * DK_B), tok(H_B * DV_B), tok(LANES), tok(H_B * DV_B),
        tok(H_C * HEAD_DIM), tok(KV_C * HEAD_DIM), tok(KV_C * HEAD_DIM),
    ]
    return pl.pallas_call(
        _proj_kernel,
        grid=(t_tok // tm,),
        in_specs=[tok(D_MODEL), const((1, D_MODEL))]
                 + [const((D_MODEL, n)) for n in _W_WIDTHS] + [
                  const((H_A, HEAD_DIM, KV_RANK)), const((1, KV_RANK))],
        out_specs=out_specs,
        out_shape=out_shape,
        compiler_params=pltpu.CompilerParams(
            dimension_semantics=("parallel",), vmem_limit_bytes=VMEM_LIMIT),
        name="norm_in_proj",
    )(x2, g, *w_windows, wuk, kvn)


def _key_to_float(k):
    k = jnp.where((k > 0) & (k < KEY_MIN_NORMAL), KEY_MIN_NORMAL, k)
    k = jnp.where((k < 0) & (k > -KEY_MIN_NORMAL), 0, k)
    bits = jnp.where(k >= 0, k, INT_MIN - k)
    return lax.bitcast_convert_type(bits, F32)


def _dsa_kernel(qidx_ref, widxt_ref, qlat_ref, kidx_ref, ckv_ref, ckvt_ref,
                biasn_ref, wuv_ref, out_ref,
                s_ref, s16_ref, lg_ref, acc_ref, da_ref, db_ref, pa_ref, pb_ref, *, s_len, topk):
    jq = pl.program_id(1)
    n_tiles = jq + 1
    n_virtual = s_len - n_tiles * QBLK
    hq = H_A * QBLK
    head_cols = [slice(hd * QBLK, (hd + 1) * QBLK) for hd in range(H_A)]

    def rows(j, r0=0, n=QBLK):
        return pl.ds(pl.multiple_of(j * QBLK + r0, n), n)

    def fold(x, op):
        return op(x.reshape(x.shape[0] // SUBLANES, SUBLANES, x.shape[-1]), axis=0)

    def pipeline(n, produce, consume, carry, bufs):
        a, b = bufs
        produce(jnp.maximum(n - 1, 0), a)

        def pair(i, carry):
            j = n - 1 - 2 * i
            produce(j - 1, b)
            carry = consume(j, a, carry)
            produce(jnp.maximum(j - 2, 0), a)
            return consume(j - 1, b, carry)

        carry = lax.fori_loop(0, n // 2, pair, carry)
        return lax.cond(n % 2 == 1, lambda c: consume(0, a, c), lambda c: c, carry)

    qis = qidx_ref[0, 0]
    wt = widxt_ref[0]

    def stage_scores(j, buf):
        buf[...] = _dot(kidx_ref[0, rows(j), :], qis)

    def put_scores(j, buf, carry, diagonal=False):
        for r0 in range(0, QBLK, ROW_CHUNK):
            rc = slice(r0, r0 + ROW_CHUNK)
            s = jnp.maximum(buf[rc, head_cols[0]], 0.0) * wt[0:1, :]
            for hd in range(1, IDX_HEADS):
                s = s + jnp.maximum(buf[rc, head_cols[hd]], 0.0) * wt[hd:hd + 1, :]
            if diagonal:
                kchunk = (r0 + lax.broadcasted_iota(jnp.int32, (ROW_CHUNK, 1), 0)) // CHUNK
                qchunk = lax.broadcasted_iota(jnp.int32, (1, QBLK), 1) // CHUNK
                s = jnp.where(kchunk <= qchunk, s, NEG_INF)
            s_ref[rows(j, r0, ROW_CHUNK), :] = s
            s16_ref[rows(j, r0, ROW_CHUNK), :] = s.astype(s16_ref.dtype)
        return carry

    stage_scores(jq, da_ref)
    put_scores(jq, da_ref, 0, diagonal=True)
    pipeline(jq, stage_scores, put_scores, 0, (da_ref, db_ref))

    def count_tiles(tile_count, zero):
        def quad(i, acc):
            j = 4 * i
            return acc + ((tile_count(j) + tile_count(j + 1)) + (tile_count(j + 2) + tile_count(j + 3)))

        acc = lax.fori_loop(0, n_tiles // 4, quad, zero)
        return lax.fori_loop(4 * (n_tiles // 4), n_tiles, lambda j, acc: acc + tile_count(j), acc)

    def virtual_hits(tf):
        return jnp.where(NEG_INF >= tf, n_virtual, 0)

    def count_ge_rounded(tf):
        tb = tf.astype(COARSE_DTYPE)
        one, zero = jnp.ones((), COARSE_DTYPE), jnp.zeros((), COARSE_DTYPE)
        packed = 2 * SUBLANES

        def tile_count(j):
            hit = jnp.where(s16_ref[rows(j), :] >= tb, one, zero)
            slabs = [hit[r0:r0 + packed, :] for r0 in range(0, QBLK, packed)]
            while len(slabs) > 1:
                slabs = [a + b for a, b in zip(slabs[0::2], slabs[1::2])]
            return slabs[0]

        acc = count_tiles(tile_count, jnp.zeros((packed, QBLK), COARSE_DTYPE))
        total = jnp.sum(acc.astype(F32), axis=0, keepdims=True).astype(jnp.int32)
        return total + virtual_hits(tf)

    def count_ge(tf):
        acc = count_tiles(lambda j: fold(jnp.where(s_ref[rows(j), :] >= tf, 1, 0), jnp.sum),
                          jnp.zeros((SUBLANES, QBLK), jnp.int32))
        return jnp.sum(acc, axis=0, keepdims=True) + virtual_hits(tf)

    def coarse_step(i, t_key):
        cand = t_key + jnp.left_shift(jnp.int32(1), 31 - i)
        return jnp.where(count_ge_rounded(_key_to_float(cand)) >= topk, cand, t_key)

    g_key = lax.fori_loop(0, 16, coarse_step, jnp.full((1, QBLK), INT_MIN, jnp.int32))

    def fine_step(i, carry):
        lo, hi, cnt_lo = carry
        mid = lo + jnp.right_shift(hi - lo, 1)
        cnt = count_ge(_key_to_float(mid))
        ok = cnt >= topk
        return jnp.where(ok, mid, lo), jnp.where(ok, hi, mid), jnp.where(ok, cnt, cnt_lo)

    lo0 = jnp.maximum(g_key - COARSE_STEP, KEY_NEG_INF)
    t_key, _, cnt_t = lax.fori_loop(
        0, FINE_STEPS, fine_step,
        (lo0, g_key + (COARSE_STEP + 1), jnp.full((1, QBLK), -1, jnp.int32)))
    tf = _key_to_float(t_key)

    has_tie = jnp.max(jnp.where((cnt_t == topk) | (tf == NEG_INF), 0, 1))

    @pl.when(has_tie > 0)
    def _():
        t_next = _key_to_float(t_key + 1)
        need = (topk - count_ge(t_next)).astype(F32)
        r = lax.broadcasted_iota(jnp.int32, (QBLK, QBLK), 0)
        c = lax.broadcasted_iota(jnp.int32, (QBLK, QBLK), 1)
        tril = (c <= r).astype(MXU_DTYPE)

        def body(j, carry):
            tile = s_ref[rows(j), :]
            eq = (tile >= tf) & jnp.logical_not(tile >= t_next)
            pre = _dot(tril, eq.astype(MXU_DTYPE)) + carry
            s_ref[rows(j), :] = jnp.where(eq & (pre > need), NEG_INF, tile)
            return pre[QBLK - 1:QBLK, :]

        lax.fori_loop(0, n_tiles, body, jnp.zeros((1, QBLK), F32))

    t_sel = jnp.maximum(tf, -F32_MAX)
    def logits_tile(j, mx):
        ck = ckv_ref[0, rows(j), :]
        near = jnp.minimum(jq - j, 2)
        parts = []
        for hd in range(H_A):
            lg = _dot(ck, qlat_ref[0, 0, :, head_cols[hd]])
            pm = mx[:, head_cols[hd]]
            for r0 in range(0, QBLK, ROW_CHUNK):
                rc = slice(r0, r0 + ROW_CHUNK)
                sel = s_ref[rows(j, r0, ROW_CHUNK), :] >= t_sel
                v = jnp.where(sel, lg[rc, :] + biasn_ref[near, rc, head_cols[hd]], MASKED)
                lg_ref[rows(j, r0, ROW_CHUNK), head_cols[hd]] = v
                pm = jnp.maximum(pm, fold(v, jnp.max))
            parts.append(pm)
        return jnp.concatenate(parts, axis=-1)

    mx = lax.fori_loop(0, n_tiles // 2,
                       lambda i, mx: logits_tile(2 * i + 1, logits_tile(2 * i, mx)),
                       jnp.full((SUBLANES, hq), MASKED, F32))
    mx = lax.cond(n_tiles % 2 == 1, lambda mx: logits_tile(jq, mx), lambda mx: mx, mx)
    m = jnp.max(mx, axis=0, keepdims=True)

    acc_ref[...] = jnp.zeros(acc_ref.shape, F32)

    def stage_probs(j, buf):
        for r0 in range(0, QBLK, ROW_CHUNK):
            p = jnp.exp(lg_ref[rows(j, r0, ROW_CHUNK), :] - m)
            buf[r0:r0 + ROW_CHUNK, :] = p.astype(buf.dtype)

    def accumulate(j, buf, carry):
        acc_ref[...] += _dot(ckvt_ref[0, j], buf[...])
        return carry

    pipeline(n_tiles, stage_probs, accumulate, 0, (pa_ref, pb_ref))

    l = acc_ref[KV_RANK:KV_RANK + 1, :]
    outs = []
    for hd in range(H_A):
        o_t = acc_ref[0:KV_RANK, head_cols[hd]] / l[:, head_cols[hd]]
        outs.append(_dot(o_t.T.astype(MXU_DTYPE), wuv_ref[hd]))
    out_ref[0] = jnp.concatenate(outs, axis=-1).astype(out_ref.dtype)


def _dsa_call(qidx, widxt, qlat, kidx, ckv, ckvt, biasn, wuv, bsz, s_len):
    topk = min(TOPK_MAX, s_len // 4)
    nblk = s_len // QBLK
    qblock_t = lambda depth: pl.BlockSpec((1, 1, depth, H_A * QBLK), lambda b, i: (b, i, 0, 0))
    const = lambda shape: pl.BlockSpec(shape, lambda b, i: (0,) * len(shape))
    return pl.pallas_call(
        functools.partial(_dsa_kernel, s_len=s_len, topk=topk),
        grid=(bsz, nblk),
        in_specs=[
            qblock_t(IDX_DIM),
            pl.BlockSpec((1, SUBLANES, QBLK), lambda b, i: (b, 0, i)),
            qblock_t(KV_RANK),
            pl.BlockSpec((1, s_len, IDX_DIM), lambda b, i: (b, 0, 0)),
            pl.BlockSpec((1, s_len, KV_RANK), lambda b, i: (b, 0, 0)),
            pl.BlockSpec((1, nblk, KV_EXT, QBLK), lambda b, i: (b, 0, 0, 0)),
            const((3, QBLK, H_A * QBLK)),
            const((H_A, KV_RANK, HEAD_DIM)),
        ],
        out_specs=pl.BlockSpec((1, QBLK, H_A * HEAD_DIM), lambda b, i: (b, i, 0)),
        out_shape=jax.ShapeDtypeStruct((bsz, s_len, H_A * HEAD_DIM), MXU_DTYPE),
        scratch_shapes=[
            pltpu.VMEM((s_len, QBLK), F32),
            pltpu.VMEM((s_len, QBLK), COARSE_DTYPE),
            pltpu.VMEM((s_len, H_A * QBLK), F32),
            pltpu.VMEM((KV_EXT, H_A * QBLK), F32),
            pltpu.VMEM((QBLK, H_A * QBLK), F32),
            pltpu.VMEM((QBLK, H_A * QBLK), F32),
            pltpu.VMEM((QBLK, H_A * QBLK), MXU_DTYPE),
            pltpu.VMEM((QBLK, H_A * QBLK), MXU_DTYPE),
        ],
        compiler_params=pltpu.CompilerParams(
            dimension_semantics=("parallel", "arbitrary"), vmem_limit_bytes=VMEM_LIMIT),
        name="dsa_mixer",
    )(qidx, widxt, qlat, kidx, ckv, ckvt, biasn, wuv)


def _split3(x):
    a = x.astype(MXU_DTYPE)
    r = x - a.astype(F32)
    b = r.astype(MXU_DTYPE)
    c = (r - b.astype(F32)).astype(MXU_DTYPE)
    return a, b, c


def _gla_kernel(qb_ref, kb_ref, vb_ref, glr_ref, rb_ref, wg_ref, bg_ref, gn_ref,
                out_ref, st_ref):
    @pl.when(pl.program_id(1) == 0)
    def _():
        st_ref[...] = jnp.zeros(st_ref.shape, F32)

    g_hi, g_lo, _ = _split3(glr_ref[0])
    w_hi, w_lo, _ = _split3(wg_ref[...])
    z = _dot(g_hi, w_hi) + (_dot(g_hi, w_lo) + _dot(g_lo, w_hi)) + bg_ref[...]
    g = (jnp.minimum(z, 0.0) - jnp.log1p(jnp.exp(-jnp.abs(z)))) / GATE_TEMP

    r = lax.broadcasted_iota(jnp.int32, (CHUNK, CHUNK), 0)
    c = lax.broadcasted_iota(jnp.int32, (CHUNK, CHUNK), 1)
    causal = c <= r
    tril = causal.astype(MXU_DTYPE)

    state = [st_ref[hd] for hd in range(H_B)]
    for ci in range(TS_GLA // CHUNK):
        rows = slice(ci * CHUNK, (ci + 1) * CHUNK)
        g1, g2, g3 = _split3(g[rows, :])
        b = _dot(tril, g1) + (_dot(tril, g2) + _dot(tril, g3))
        b_last = b[CHUNK - 1:CHUNK, :]
        q = qb_ref[0, rows, :].astype(F32) * (DK_B ** -0.5)
        k = kb_ref[0, rows, :].astype(F32)
        qe = (q * jnp.exp(b)).astype(MXU_DTYPE)
        ke = (k * jnp.exp(-b)).astype(MXU_DTYPE)
        kd = (k * jnp.exp(b_last - b)).astype(MXU_DTYPE)
        decay = jnp.exp(b_last)
        for hd in range(H_B):
            ks = slice(hd * DK_B, (hd + 1) * DK_B)
            vs = slice(hd * DV_B, (hd + 1) * DV_B)
            v = vb_ref[0, rows, vs]
            att = jnp.where(causal, _dot_nt(qe[:, ks], ke[:, ks]), 0.0)
            st = state[hd]
            o = _dot(att.astype(MXU_DTYPE), v) + _dot_nt(qe[:, ks], st.astype(MXU_DTYPE))
            state[hd] = st * decay[:, ks] + _dot_tn(v, kd[:, ks])
            o = o * lax.rsqrt(jnp.mean(o * o, axis=-1, keepdims=True) + EPS) * gn_ref[:, vs]
            rg = rb_ref[0, rows, vs].astype(F32)
            o = o * (rg / (1.0 + jnp.exp(-rg)))
            out_ref[0, rows, vs] = o.astype(out_ref.dtype)
    for hd in range(H_B):
        st_ref[hd] = state[hd]


def _gla_call(qb, kb, vb, glr, rb, wg, bg, gn, bsz, s_len):
    ts = TS_GLA
    tok = lambda width: pl.BlockSpec((1, ts, width), lambda b, i: (b, i, 0))
    const = lambda shape: pl.BlockSpec(shape, lambda b, i: (0,) * len(shape))
    return pl.pallas_call(
        _gla_kernel,
        grid=(bsz, s_len // ts),
        in_specs=[tok(H_B * DK_B), tok(H_B * DK_B), tok(H_B * DV_B), tok(LANES), tok(H_B * DV_B),
                  const((LANES, H_B * DK_B)), const((1, H_B * DK_B)), const((1, H_B * DV_B))],
        out_specs=tok(H_B * DV_B),
        out_shape=jax.ShapeDtypeStruct((bsz, s_len, H_B * DV_B), MXU_DTYPE),
        scratch_shapes=[pltpu.VMEM((H_B, DV_B, DK_B), F32)],
        compiler_params=pltpu.CompilerParams(
            dimension_semantics=("parallel", "arbitrary"), vmem_limit_bytes=VMEM_LIMIT),
        name="gla_mixer",
    )(qb, kb, vb, glr, rb, wg, bg, gn)


SWA_SUB = WINDOW
SWA_WIN = SWA_SUB + WINDOW


def _swa_window_start(tile_start, sub):
    return jnp.maximum(tile_start + sub * SWA_SUB - WINDOW, 0)


def _swa_kernel(sink_ref, qc_ref, kc_ref, vc_ref, mb_ref, out_ref):
    i = pl.program_id(1)
    grp = H_C // KV_C
    for sub in range(TQ_SWA // SWA_SUB):
        qrows = slice(sub * SWA_SUB, (sub + 1) * SWA_SUB)
        start = pl.multiple_of(_swa_window_start(i * TQ_SWA, sub), WINDOW)
        k = kc_ref[0, pl.ds(start, SWA_WIN), :]
        v = vc_ref[0, pl.ds(start, SWA_WIN), :]
        outs = []
        for hd in range(H_C):
            kv = hd // grp
            q = qc_ref[0, qrows, hd * HEAD_DIM:(hd + 1) * HEAD_DIM]
            lg = (_dot_nt(q, k[:, kv * HEAD_DIM:(kv + 1) * HEAD_DIM]) * (HEAD_DIM ** -0.5)
                  + mb_ref[0, hd, sub])
            sink = sink_ref[hd]
            m = jnp.maximum(jnp.max(lg, axis=-1, keepdims=True), sink)
            e = jnp.exp(lg - m)
            p = e / (jnp.sum(e, axis=-1, keepdims=True) + jnp.exp(sink - m))
            outs.append(_dot(p.astype(MXU_DTYPE), v[:, kv * HEAD_DIM:(kv + 1) * HEAD_DIM]))
        out_ref[0, qrows, :] = jnp.concatenate(outs, axis=-1).astype(out_ref.dtype)


def _swa_call(sinks, qc, kc, vc, maskbias, bsz, s_len):
    tq = TQ_SWA
    return pl.pallas_call(
        _swa_kernel,
        grid=(bsz, s_len // tq),
        in_specs=[
            pl.BlockSpec(memory_space=pltpu.SMEM),
            pl.BlockSpec((1, tq, H_C * HEAD_DIM), lambda b, i: (b, i, 0)),
            pl.BlockSpec((1, s_len, KV_C * HEAD_DIM), lambda b, i: (b, 0, 0)),
            pl.BlockSpec((1, s_len, KV_C * HEAD_DIM), lambda b, i: (b, 0, 0)),
            pl.BlockSpec((1, H_C, tq // SWA_SUB, SWA_SUB, SWA_WIN),
                         lambda b, i: (jnp.minimum(i, 1), 0, 0, 0, 0)),
        ],
        out_specs=pl.BlockSpec((1, tq, H_C * HEAD_DIM), lambda b, i: (b, i, 0)),
        out_shape=jax.ShapeDtypeStruct((bsz, s_len, H_C * HEAD_DIM), MXU_DTYPE),
        compiler_params=pltpu.CompilerParams(
            dimension_semantics=("parallel", "arbitrary"), vmem_limit_bytes=VMEM_LIMIT),
        name="swa_mixer",
    )(sinks, qc, kc, vc, maskbias)


def _ffn_kernel(x_ref, oa_ref, ob_ref, oc_ref, gf_ref, gfin_ref,
                wo_hbm, wg_hbm, wu_hbm, wd_hbm, out_ref,
                wo_ref, wg_ref, wu_ref, wd_ref, sems, *, final_norm):
    @pl.when(pl.program_id(0) == 0)
    def _():
        pairs = ((wo_hbm, wo_ref), (wg_hbm, wg_ref), (wu_hbm, wu_ref), (wd_hbm, wd_ref))
        copies = [pltpu.make_async_copy(src, dst, sems.at[k]) for k, (src, dst) in enumerate(pairs)]
        for cp in copies:
            cp.start()
        for cp in copies:
            cp.wait()

    na = H_A * HEAD_DIM
    nb = H_B * DV_B
    x1 = (x_ref[...] + _dot(oa_ref[...], wo_ref[0:na, :])
          + _dot(ob_ref[...], wo_ref[na:na + nb, :])
          + _dot(oc_ref[...], wo_ref[na + nb:, :]))
    h = _rms(x1, gf_ref[...]).astype(MXU_DTYPE)
    y = x1
    for j in range(D_FF // TF_FFN):
        cols = slice(j * TF_FFN, (j + 1) * TF_FFN)
        gate = _dot(h, wg_ref[:, cols])
        up = _dot(h, wu_ref[:, cols])
        a = (gate / (1.0 + jnp.exp(-gate))) * up
        y = y + _dot(a.astype(MXU_DTYPE), wd_ref[cols, :])
    if final_norm:
        y = _rms(y, gfin_ref[...])
    out_ref[...] = y


def _ffn_call(x2, oa, ob, oc, wo, gf, wg, wu, wd, gfin, final_norm):
    t_tok = x2.shape[0]
    tm = TM_FFN
    tok = lambda width: pl.BlockSpec((tm, width), lambda i: (i, 0))
    const = lambda shape: pl.BlockSpec(shape, lambda i: (0,) * len(shape))
    in_hbm = pl.BlockSpec(memory_space=pl.ANY)
    return pl.pallas_call(
        functools.partial(_ffn_kernel, final_norm=final_norm),
        grid=(t_tok // tm,),
        in_specs=[tok(D_MODEL), tok(H_A * HEAD_DIM), tok(H_B * DV_B), tok(H_C * HEAD_DIM),
                  const((1, D_MODEL)), const((1, D_MODEL)), in_hbm, in_hbm, in_hbm, in_hbm],
        out_specs=tok(D_MODEL),
        out_shape=jax.ShapeDtypeStruct((t_tok, D_MODEL), F32),
        scratch_shapes=[pltpu.VMEM((D_MODEL, D_MODEL), MXU_DTYPE),
                        pltpu.VMEM((D_MODEL, D_FF), MXU_DTYPE),
                        pltpu.VMEM((D_MODEL, D_FF), MXU_DTYPE),
                        pltpu.VMEM((D_FF, D_MODEL), MXU_DTYPE),
                        pltpu.SemaphoreType.DMA((4,))],
        compiler_params=pltpu.CompilerParams(
            dimension_semantics=("arbitrary",), vmem_limit_bytes=VMEM_LIMIT),
        name="out_proj_ffn",
    )(x2, oa, ob, oc, gf, gfin, wo, wg, wu, wd)


def _rel_bucket(rel):
    nb = NUM_BUCKETS // 2
    max_exact = nb // 2
    n = jnp.abs(rel)
    side = jnp.where(rel > 0, nb, 0)
    nf = jnp.maximum(n, 1).astype(jnp.float32)
    large = max_exact + (jnp.log(nf / max_exact) / math.log(MAX_DISTANCE / max_exact)
                         * (nb - max_exact)).astype(jnp.int32)
    large = jnp.minimum(large, nb - 1)
    return side + jnp.where(n < max_exact, n, large)


def _bucket_ids(ndim, axis):
    shape = [1] * ndim
    shape[axis] = NUM_BUCKETS
    return jnp.arange(NUM_BUCKETS, dtype=jnp.int32).reshape(shape)


def _dsa_bias_tables(rel_bias):
    table = rel_bias[:, :H_A].astype(F32)
    far = _rel_bucket(jnp.full((), -2 * QBLK, jnp.int32))
    table = table - jnp.sum(jnp.where(_bucket_ids(2, 0) == far, table, 0.0), axis=0, keepdims=True)
    kk = jnp.arange(QBLK, dtype=jnp.int32)[:, None]
    qq = jnp.arange(QBLK, dtype=jnp.int32)[None, :]
    near = []
    for shift in (0, QBLK):
        bucket = _rel_bucket(kk - shift - qq)[:, None, None, :]
        b = jnp.sum(jnp.where(bucket == _bucket_ids(4, 1), table[None, :, :, None], 0.0), axis=1)
        near.append(b.reshape(QBLK, H_A * QBLK))
    near.append(jnp.zeros((QBLK, H_A * QBLK), F32))
    return jnp.stack(near)


def _swa_maskbias(rel_bias):
    table = rel_bias[:, H_A:].astype(F32)
    out = []
    for tile_start in (0, TQ_SWA):
        subs = []
        for sub in range(TQ_SWA // SWA_SUB):
            qq = tile_start + sub * SWA_SUB + jnp.arange(SWA_SUB, dtype=jnp.int32)[:, None]
            kk = (max(tile_start + sub * SWA_SUB - WINDOW, 0)
                  + jnp.arange(SWA_WIN, dtype=jnp.int32)[None, :])
            dc = qq // CHUNK - kk // CHUNK
            band = (dc >= 0) & (dc <= WIN_CHUNKS)
            bucket = _rel_bucket(kk - qq)[None, None]
            b = jnp.sum(jnp.where(bucket == _bucket_ids(4, 0), table[:, :, None, None], 0.0), axis=0)
            subs.append(jnp.where(band[None], b, NEG_INF))
        out.append(jnp.stack(subs, axis=1))
    return jnp.stack(out)


def kernel(x, w_in, w_out, norm_mix, norm_ffn, kv_norm, w_uk, w_uv, w_gate2, b_gate,
           gla_norm, sinks, rel_bias, w_ffn_gate, w_ffn_up, w_ffn_down, final_norm):
    bsz, s_len, _ = x.shape
    t_tok = bsz * s_len
    wd = MXU_DTYPE
    biasn = _dsa_bias_tables(rel_bias)
    maskbias = _swa_maskbias(rel_bias)
    x2 = x.reshape(t_tok, D_MODEL)
    row = lambda v: v.reshape(1, -1).astype(F32)
    for l in range(DEPTH):
        w_windows = _split_w_in(w_in[l].astype(wd))
        wuk = jnp.transpose(w_uk[l], (1, 2, 0)).astype(wd)
        wuv = jnp.transpose(w_uv[l], (1, 0, 2)).astype(wd)
        (qlat, ckv, ckvt, qidx, kidx, widxt, qb, kb, vb, glr, rb, qc, kc, vc) = _proj_call(
            x2, row(norm_mix[l]), w_windows, wuk, row(kv_norm[l]), bsz, s_len)
        b3 = lambda a: a.reshape(bsz, s_len, a.shape[-1])
        out_a = _dsa_call(qidx, widxt, qlat, b3(kidx), b3(ckv), ckvt,
                          biasn, wuv, bsz, s_len)
        wg2 = jnp.zeros((LANES, H_B * DK_B), F32).at[:GATE_RANK].set(w_gate2[l])
        out_b = _gla_call(b3(qb), b3(kb), b3(vb), b3(glr), b3(rb), wg2,
                          row(b_gate[l]), row(gla_norm[l]), bsz, s_len)
        out_c = _swa_call(sinks[l].astype(F32), b3(qc), b3(kc), b3(vc), maskbias, bsz, s_len)
        f2 = lambda a: a.reshape(t_tok, a.shape[-1])
        x2 = _ffn_call(x2, f2(out_a), f2(out_b), f2(out_c), w_out[l].astype(wd),
                       row(norm_ffn[l]), w_ffn_gate[l].astype(wd), w_ffn_up[l].astype(wd),
                       w_ffn_down[l].astype(wd), row(final_norm), l == DEPTH - 1)
    return x2.reshape(bsz, s_len, D_MODEL)
```

```python
import functools
import math

import jax
import jax.numpy as jnp
from jax import lax
from jax.experimental import pallas as pl
from jax.experimental.pallas import tpu as pltpu

D_MODEL = 1024
DEPTH = 2
CHUNK = 64
HEAD_DIM = 64
H_A = 4
KV_RANK = 128
IDX_HEADS = 4
IDX_DIM = 64
TOPK_MAX = 256
H_B = 4
DK_B = 64
DV_B = 128
GATE_RANK = 16
GATE_TEMP = 16.0
H_C = 4
KV_C = 2
WINDOW = 128
WIN_CHUNKS = WINDOW // CHUNK
NUM_BUCKETS = 32
MAX_DISTANCE = 128
D_FF = ((8 * D_MODEL // 3 + 255) // 256) * 256
EPS = 1e-6

IN_SPLITS = (H_A * HEAD_DIM, KV_RANK, IDX_HEADS * IDX_DIM, IDX_DIM, IDX_HEADS,
             H_B * DK_B, H_B * DK_B, H_B * DV_B, GATE_RANK, H_B * DV_B,
             H_C * HEAD_DIM, KV_C * HEAD_DIM, KV_C * HEAD_DIM)

LANES = 128
SUBLANES = 8
VMEM_LIMIT = 56 * 1024 * 1024

MXU_DTYPE = jnp.bfloat16
HALF_DTYPE = jnp.int16
F32 = jnp.float32
NEG_INF = float("-inf")
MASKED = -1e30
F32_MAX = float(jnp.finfo(jnp.float32).max)
INT_MIN = -2 ** 31
KEY_NEG_INF = INT_MIN + 2 ** 23
KEY_MIN_NORMAL = 2 ** 23
HALF_MIN = -2 ** 15

QBLK = 256
KV_EXT = KV_RANK + 16
ROW_CHUNK = 32
TM_PROJ = 512
TS_GLA = 256
TQ_SWA = 256
TM_FFN = 512
TF_FFN = D_FF


def _dot(a, b):
    return jnp.dot(a, b, preferred_element_type=F32)


def _dot_nt(a, b):
    return lax.dot_general(a, b, (((1,), (1,)), ((), ())), preferred_element_type=F32)


def _dot_tn(a, b):
    return lax.dot_general(a, b, (((0,), (0,)), ((), ())), preferred_element_type=F32)


def _rms(x, g):
    return x * lax.rsqrt(jnp.mean(x * x, axis=-1, keepdims=True) + EPS) * g


_W_STARTS = (0, sum(IN_SPLITS[:5]), sum(IN_SPLITS[:9]))
_W_WIDTHS = (768, 1152, 1024)
_C_QA = (0, 0)
_C_CKV = (0, 256)
_C_QIDX = (0, 384)
_C_KW = (0, 640)
_C_QB = (1, 0)
_C_KB = (1, 256)
_C_VB = (1, 512)
_C_GLR = (1, 1024)
_C_RB = (2, 0)
_C_QC = (2, 512)
_C_KC = (2, 768)
_C_VC = (2, 896)


def _proj_kernel(x_ref, g_ref, wa_ref, wb_ref, wc_ref, wuk_ref, kvn_ref,
                 qlat_ref, ckv_ref, ckvt_ref, qidx_ref, kidx_ref, widxt_ref,
                 qb_ref, kb_ref, vb_ref, glr_ref, rb_ref, qc_ref, kc_ref, vc_ref):
    h = _rms(x_ref[...], g_ref[...]).astype(MXU_DTYPE)

    p = [_dot(h, w_ref[...]) for w_ref in (wa_ref, wb_ref, wc_ref)]

    def seg(where, width):
        window, c0 = where
        return p[window][:, c0:c0 + width]

    qa = seg(_C_QA, 256)
    for hd in range(H_A):
        qh = qa[:, hd * HEAD_DIM:(hd + 1) * HEAD_DIM].astype(MXU_DTYPE)
        ql = _dot(qh, wuk_ref[hd]) * (HEAD_DIM ** -0.5)
        for t in range(TM_PROJ // QBLK):
            qlat_ref[0, t, :, hd * QBLK:(hd + 1) * QBLK] = (
                ql[t * QBLK:(t + 1) * QBLK, :].T.astype(qlat_ref.dtype))

    ckv = _rms(seg(_C_CKV, KV_RANK), kvn_ref[...])
    ckv_ref[...] = ckv.astype(ckv_ref.dtype)
    for t in range(TM_PROJ // QBLK):
        ckvt_ref[0, t, 0:KV_RANK, :] = ckv[t * QBLK:(t + 1) * QBLK, :].T.astype(ckvt_ref.dtype)
        ckvt_ref[0, t, KV_RANK:, :] = jnp.ones((KV_EXT - KV_RANK, QBLK), ckvt_ref.dtype)

    qi = seg(_C_QIDX, 256)
    for t in range(TM_PROJ // QBLK):
        qit = qi[t * QBLK:(t + 1) * QBLK, :].T
        for hd in range(IDX_HEADS):
            qidx_ref[0, t, :, hd * QBLK:(hd + 1) * QBLK] = (
                qit[hd * IDX_DIM:(hd + 1) * IDX_DIM, :].astype(qidx_ref.dtype))
    kw = seg(_C_KW, LANES)
    kidx_ref[...] = kw[:, :IDX_DIM].astype(kidx_ref.dtype)
    kwt = kw.T
    widxt_ref[0] = kwt[IDX_DIM:IDX_DIM + SUBLANES, :] * (IDX_HEADS ** -0.5 * IDX_DIM ** -0.5)

    qb_ref[...] = seg(_C_QB, 256).astype(qb_ref.dtype)
    kb_ref[...] = seg(_C_KB, 256).astype(kb_ref.dtype)
    vb_ref[...] = seg(_C_VB, 512).astype(vb_ref.dtype)
    glr_ref[...] = seg(_C_GLR, LANES)
    rb_ref[...] = seg(_C_RB, 512).astype(rb_ref.dtype)
    qc_ref[...] = seg(_C_QC, 256).astype(qc_ref.dtype)
    kc_ref[...] = seg(_C_KC, LANES).astype(kc_ref.dtype)
    vc_ref[...] = seg(_C_VC, LANES).astype(vc_ref.dtype)


def _split_w_in(w):
    return tuple(w[:, c0:c0 + n] for c0, n in zip(_W_STARTS, _W_WIDTHS))


def _proj_call(x2, g, w_windows, wuk, kvn, bsz, s_len):
    t_tok = x2.shape[0]
    tm = TM_PROJ
    n_s = s_len // tm
    act = MXU_DTYPE

    def tok(width):
        return pl.BlockSpec((tm, width), lambda i: (i, 0))

    def const(shape):
        return pl.BlockSpec(shape, lambda i: (0,) * len(shape))

    def qblocks(depth):
        return pl.BlockSpec((1, tm // QBLK, depth, H_A * QBLK), lambda i: (i // n_s, i % n_s, 0, 0))

    out_shape = [
        jax.ShapeDtypeStruct((bsz, s_len // QBLK, KV_RANK, H_A * QBLK), act),
        jax.ShapeDtypeStruct((t_tok, KV_RANK), act),
        jax.ShapeDtypeStruct((bsz, s_len // QBLK, KV_EXT, QBLK), act),
        jax.ShapeDtypeStruct((bsz, s_len // QBLK, IDX_DIM, IDX_HEADS * QBLK), act),
        jax.ShapeDtypeStruct((t_tok, IDX_DIM), act),
        jax.ShapeDtypeStruct((bsz, SUBLANES, s_len), F32),
        jax.ShapeDtypeStruct((t_tok, H_B * DK_B), act),
        jax.ShapeDtypeStruct((t_tok, H_B * DK_B), act),
        jax.ShapeDtypeStruct((t_tok, H_B * DV_B), act),
        jax.ShapeDtypeStruct((t_tok, LANES), F32),
        jax.ShapeDtypeStruct((t_tok, H_B * DV_B), act),
        jax.ShapeDtypeStruct((t_tok, H_C * HEAD_DIM), act),
        jax.ShapeDtypeStruct((t_tok, KV_C * HEAD_DIM), act),
        jax.ShapeDtypeStruct((t_tok, KV_C * HEAD_DIM), act),
    ]
    out_specs = [
        qblocks(KV_RANK), tok(KV_RANK),
        pl.BlockSpec((1, tm // QBLK, KV_EXT, QBLK), lambda i: (i // n_s, i % n_s, 0, 0)),
        qblocks(IDX_DIM), tok(IDX_DIM),
        pl.BlockSpec((1, SUBLANES, tm), lambda i: (i // n_s, 0, i % n_s)),
        tok(H_B * DK_B), tok(H_B * DK_B), tok(H_B * DV_B), tok(LANES), tok(H_B * DV_B),
        tok(H_C * HEAD_DIM), tok(KV_C * HEAD_DIM), tok(KV_C * HEAD_DIM),
    ]
    return pl.pallas_call(
        _proj_kernel,
        grid=(t_tok // tm,),
        in_specs=[tok(D_MODEL), const((1, D_MODEL))]
                 + [const((D_MODEL, n)) for n in _W_WIDTHS] + [
                  const((H_A, HEAD_DIM, KV_RANK)), const((1, KV_RANK))],
        out_specs=out_specs,
        out_shape=out_shape,
        compiler_params=pltpu.CompilerParams(
            dimension_semantics=("parallel",), vmem_limit_bytes=VMEM_LIMIT),
        name="norm_in_proj",
    )(x2, g, *w_windows, wuk, kvn)


def _float_to_key(x):
    bits = lax.bitcast_convert_type(x, jnp.int32)
    return jnp.where(bits >= 0, bits, INT_MIN - bits)


def _key_to_float(k):
    k = jnp.where((k > 0) & (k < KEY_MIN_NORMAL), KEY_MIN_NORMAL, k)
    k = jnp.where((k < 0) & (k > -KEY_MIN_NORMAL), 0, k)
    bits = jnp.where(k >= 0, k, INT_MIN - k)
    return lax.bitcast_convert_type(bits, F32)


def _dsa_kernel(qidx_ref, widxt_ref, qlat_ref, kidx_ref, ckv_ref, ckvt_ref,
                biasn_ref, wuv_ref, out_ref,
                s_ref, hi_ref, lo_ref, lg_ref, acc_ref, da_ref, db_ref, pa_ref, pb_ref,
                da16_ref, *, s_len, topk):
    jq = pl.program_id(1)
    n_tiles = jq + 1
    n_virtual = s_len - n_tiles * QBLK
    hq = H_A * QBLK
    head_cols = [slice(hd * QBLK, (hd + 1) * QBLK) for hd in range(H_A)]

    def rows(j, r0=0, n=QBLK):
        return pl.ds(pl.multiple_of(j * QBLK + r0, n), n)

    def fold(x, op):
        return op(x.reshape(x.shape[0] // SUBLANES, SUBLANES, x.shape[-1]), axis=0)

    def pipeline(n, produce, consume, carry, bufs):
        a, b = bufs
        produce(jnp.maximum(n - 1, 0), a)

        def pair(i, carry):
            j = n - 1 - 2 * i
            produce(j - 1, b)
            carry = consume(j, a, carry)
            produce(jnp.maximum(j - 2, 0), a)
            return consume(j - 1, b, carry)

        carry = lax.fori_loop(0, n // 2, pair, carry)
        return lax.cond(n % 2 == 1, lambda c: consume(0, a, c), lambda c: c, carry)

    qis = qidx_ref[0, 0]
    wt = widxt_ref[0]

    def stage_scores(j, buf):
        buf[...] = _dot(kidx_ref[0, rows(j), :], qis)

    def put_scores(j, buf, carry, diagonal=False):
        for r0 in range(0, QBLK, ROW_CHUNK):
            rc = slice(r0, r0 + ROW_CHUNK)
            s = jnp.maximum(buf[rc, head_cols[0]], 0.0) * wt[0:1, :]
            for hd in range(1, IDX_HEADS):
                s = s + jnp.maximum(buf[rc, head_cols[hd]], 0.0) * wt[hd:hd + 1, :]
            if diagonal:
                kchunk = (r0 + lax.broadcasted_iota(jnp.int32, (ROW_CHUNK, 1), 0)) // CHUNK
                qchunk = lax.broadcasted_iota(jnp.int32, (1, QBLK), 1) // CHUNK
                s = jnp.where(kchunk <= qchunk, s, NEG_INF)
            s_ref[rows(j, r0, ROW_CHUNK), :] = s
            key = _float_to_key(s)
            hi_ref[rows(j, r0, ROW_CHUNK), :] = jnp.right_shift(key, 16).astype(HALF_DTYPE)
            lo_ref[rows(j, r0, ROW_CHUNK), :] = (jnp.bitwise_and(key, 0xFFFF) + HALF_MIN).astype(HALF_DTYPE)
        return carry

    stage_scores(jq, da_ref)
    put_scores(jq, da_ref, 0, diagonal=True)
    pipeline(jq, stage_scores, put_scores, 0, (da_ref, db_ref))

    def count_tiles(tile_count, zero):
        def quad(i, acc):
            j = 4 * i
            return acc + ((tile_count(j) + tile_count(j + 1)) + (tile_count(j + 2) + tile_count(j + 3)))

        acc = lax.fori_loop(0, n_tiles // 4, quad, zero)
        return lax.fori_loop(4 * (n_tiles // 4), n_tiles, lambda j, acc: acc + tile_count(j), acc)

    packed = 2 * SUBLANES

    def count_half(ref, t, strict=False):
        th = t.astype(HALF_DTYPE)
        one, zero = jnp.ones((), HALF_DTYPE), jnp.zeros((), HALF_DTYPE)

        def tile_count(j):
            x = ref[rows(j), :]
            hit = jnp.where(x > th if strict else x >= th, one, zero)
            slabs = [hit[r0:r0 + packed, :] for r0 in range(0, QBLK, packed)]
            while len(slabs) > 1:
                slabs = [a + b for a, b in zip(slabs[0::2], slabs[1::2])]
            return slabs[0]

        acc = count_tiles(tile_count, jnp.zeros((packed, QBLK), HALF_DTYPE))
        return jnp.sum(acc.astype(jnp.int32), axis=0, keepdims=True)

    def bisect_half(count_at_least, target):
        def step(i, t):
            cand = t + jnp.left_shift(jnp.int32(1), 15 - i)
            return jnp.where(count_at_least(cand) >= target, cand, t)

        return lax.fori_loop(0, 16, step, jnp.full((1, QBLK), HALF_MIN, jnp.int32))

    hi_neg_inf = KEY_NEG_INF >> 16
    g_hi = bisect_half(
        lambda t: count_half(hi_ref, t) + jnp.where(t <= hi_neg_inf, n_virtual, 0), topk)
    above = count_half(hi_ref, g_hi, strict=True) + jnp.where(g_hi < hi_neg_inf, n_virtual, 0)

    def mark_candidates(j, carry):
        g = g_hi.astype(HALF_DTYPE)
        da16_ref[rows(j), :] = jnp.where(hi_ref[rows(j), :] == g, lo_ref[rows(j), :],
                                         jnp.full((), HALF_MIN, HALF_DTYPE))
        return carry

    lax.fori_loop(0, n_tiles, mark_candidates, 0)
    g_lo = bisect_half(lambda t: count_half(da16_ref, t), topk - above)
    t_key = g_hi * 65536 + (g_lo - HALF_MIN)
    tf = _key_to_float(t_key)

    def count_ge(tf):
        acc = count_tiles(lambda j: fold(jnp.where(s_ref[rows(j), :] >= tf, 1, 0), jnp.sum),
                          jnp.zeros((SUBLANES, QBLK), jnp.int32))
        return jnp.sum(acc, axis=0, keepdims=True) + jnp.where(NEG_INF >= tf, n_virtual, 0)

    cnt_t = count_ge(tf)

    has_tie = jnp.max(jnp.where((cnt_t == topk) | (tf == NEG_INF), 0, 1))

    @pl.when(has_tie > 0)
    def _():
        t_next = _key_to_float(t_key + 1)
        need = (topk - count_ge(t_next)).astype(F32)
        r = lax.broadcasted_iota(jnp.int32, (QBLK, QBLK), 0)
        c = lax.broadcasted_iota(jnp.int32, (QBLK, QBLK), 1)
        tril = (c <= r).astype(MXU_DTYPE)

        def body(j, carry):
            tile = s_ref[rows(j), :]
            eq = (tile >= tf) & jnp.logical_not(tile >= t_next)
            pre = _dot(tril, eq.astype(MXU_DTYPE)) + carry
            s_ref[rows(j), :] = jnp.where(eq & (pre > need), NEG_INF, tile)
            return pre[QBLK - 1:QBLK, :]

        lax.fori_loop(0, n_tiles, body, jnp.zeros((1, QBLK), F32))

    t_sel = jnp.maximum(tf, -F32_MAX)
    def logits_tile(j, mx):
        ck = ckv_ref[0, rows(j), :]
        near = jnp.minimum(jq - j, 2)
        parts = []
        for hd in range(H_A):
            lg = _dot(ck, qlat_ref[0, 0, :, head_cols[hd]])
            pm = mx[:, head_cols[hd]]
            for r0 in range(0, QBLK, ROW_CHUNK):
                rc = slice(r0, r0 + ROW_CHUNK)
                sel = s_ref[rows(j, r0, ROW_CHUNK), :] >= t_sel
                v = jnp.where(sel, lg[rc, :] + biasn_ref[near, rc, head_cols[hd]], MASKED)
                lg_ref[rows(j, r0, ROW_CHUNK), head_cols[hd]] = v
                pm = jnp.maximum(pm, fold(v, jnp.max))
            parts.append(pm)
        return jnp.concatenate(parts, axis=-1)

    mx = lax.fori_loop(0, n_tiles // 2,
                       lambda i, mx: logits_tile(2 * i + 1, logits_tile(2 * i, mx)),
                       jnp.full((SUBLANES, hq), MASKED, F32))
    mx = lax.cond(n_tiles % 2 == 1, lambda mx: logits_tile(jq, mx), lambda mx: mx, mx)
    m = jnp.max(mx, axis=0, keepdims=True)

    acc_ref[...] = jnp.zeros(acc_ref.shape, F32)

    def stage_probs(j, buf):
        for r0 in range(0, QBLK, ROW_CHUNK):
            p = jnp.exp(lg_ref[rows(j, r0, ROW_CHUNK), :] - m)
            buf[r0:r0 + ROW_CHUNK, :] = p.astype(buf.dtype)

    def accumulate(j, buf, carry):
        acc_ref[...] += _dot(ckvt_ref[0, j], buf[...])
        return carry

    pipeline(n_tiles, stage_probs, accumulate, 0, (pa_ref, pb_ref))

    l = acc_ref[KV_RANK:KV_RANK + 1, :]
    outs = []
    for hd in range(H_A):
        o_t = acc_ref[0:KV_RANK, head_cols[hd]] / l[:, head_cols[hd]]
        outs.append(_dot(o_t.T.astype(MXU_DTYPE), wuv_ref[hd]))
    out_ref[0] = jnp.concatenate(outs, axis=-1).astype(out_ref.dtype)


def _dsa_call(qidx, widxt, qlat, kidx, ckv, ckvt, biasn, wuv, bsz, s_len):
    topk = min(TOPK_MAX, s_len // 4)
    nblk = s_len // QBLK
    qblock_t = lambda depth: pl.BlockSpec((1, 1, depth, H_A * QBLK), lambda b, i: (b, i, 0, 0))
    const = lambda shape: pl.BlockSpec(shape, lambda b, i: (0,) * len(shape))
    return pl.pallas_call(
        functools.partial(_dsa_kernel, s_len=s_len, topk=topk),
        grid=(bsz, nblk),
        in_specs=[
            qblock_t(IDX_DIM),
            pl.BlockSpec((1, SUBLANES, QBLK), lambda b, i: (b, 0, i)),
            qblock_t(KV_RANK),
            pl.BlockSpec((1, s_len, IDX_DIM), lambda b, i: (b, 0, 0)),
            pl.BlockSpec((1, s_len, KV_RANK), lambda b, i: (b, 0, 0)),
            pl.BlockSpec((1, nblk, KV_EXT, QBLK), lambda b, i: (b, 0, 0, 0)),
            const((3, QBLK, H_A * QBLK)),
            const((H_A, KV_RANK, HEAD_DIM)),
        ],
        out_specs=pl.BlockSpec((1, QBLK, H_A * HEAD_DIM), lambda b, i: (b, i, 0)),
        out_shape=jax.ShapeDtypeStruct((bsz, s_len, H_A * HEAD_DIM), MXU_DTYPE),
        scratch_shapes=[
            pltpu.VMEM((s_len, QBLK), F32),
            pltpu.VMEM((s_len, QBLK), HALF_DTYPE),
            pltpu.VMEM((s_len, QBLK), HALF_DTYPE),
            pltpu.VMEM((s_len, H_A * QBLK), F32),
            pltpu.VMEM((KV_EXT, H_A * QBLK), F32),
            pltpu.VMEM((QBLK, H_A * QBLK), F32),
            pltpu.VMEM((QBLK, H_A * QBLK), F32),
            pltpu.VMEM((QBLK, H_A * QBLK), MXU_DTYPE),
            pltpu.VMEM((QBLK, H_A * QBLK), MXU_DTYPE),
            pltpu.VMEM((s_len, QBLK), HALF_DTYPE),
        ],
        compiler_params=pltpu.CompilerParams(
            dimension_semantics=("parallel", "arbitrary"), vmem_limit_bytes=VMEM_LIMIT),
        name="dsa_mixer",
    )(qidx, widxt, qlat, kidx, ckv, ckvt, biasn, wuv)


def _split3(x):
    a = x.astype(MXU_DTYPE)
    r = x - a.astype(F32)
    b = r.astype(MXU_DTYPE)
    c = (r - b.astype(F32)).astype(MXU_DTYPE)
    return a, b, c


def _gla_kernel(qb_ref, kb_ref, vb_ref, glr_ref, rb_ref, wg_ref, bg_ref, gn_ref,
                out_ref, st_ref):
    @pl.when(pl.program_id(1) == 0)
    def _():
        st_ref[...] = jnp.zeros(st_ref.shape, F32)

    g_hi, g_lo, _ = _split3(glr_ref[0])
    w_hi, w_lo, _ = _split3(wg_ref[...])
    z = _dot(g_hi, w_hi) + (_dot(g_hi, w_lo) + _dot(g_lo, w_hi)) + bg_ref[...]
    g = (jnp.minimum(z, 0.0) - jnp.log1p(jnp.exp(-jnp.abs(z)))) / GATE_TEMP

    r = lax.broadcasted_iota(jnp.int32, (CHUNK, CHUNK), 0)
    c = lax.broadcasted_iota(jnp.int32, (CHUNK, CHUNK), 1)
    causal = c <= r
    tril = causal.astype(MXU_DTYPE)

    state = [st_ref[hd] for hd in range(H_B)]
    for ci in range(TS_GLA // CHUNK):
        rows = slice(ci * CHUNK, (ci + 1) * CHUNK)
        g1, g2, g3 = _split3(g[rows, :])
        b = _dot(tril, g1) + (_dot(tril, g2) + _dot(tril, g3))
        b_last = b[CHUNK - 1:CHUNK, :]
        q = qb_ref[0, rows, :].astype(F32) * (DK_B ** -0.5)
        k = kb_ref[0, rows, :].astype(F32)
        qe = (q * jnp.exp(b)).astype(MXU_DTYPE)
        ke = (k * jnp.exp(-b)).astype(MXU_DTYPE)
        kd = (k * jnp.exp(b_last - b)).astype(MXU_DTYPE)
        decay = jnp.exp(b_last)
        for hd in range(H_B):
            ks = slice(hd * DK_B, (hd + 1) * DK_B)
            vs = slice(hd * DV_B, (hd + 1) * DV_B)
            v = vb_ref[0, rows, vs]
            att = jnp.where(causal, _dot_nt(qe[:, ks], ke[:, ks]), 0.0)
            st = state[hd]
            o = _dot(att.astype(MXU_DTYPE), v) + _dot_nt(qe[:, ks], st.astype(MXU_DTYPE))
            state[hd] = st * decay[:, ks] + _dot_tn(v, kd[:, ks])
            o = o * lax.rsqrt(jnp.mean(o * o, axis=-1, keepdims=True) + EPS) * gn_ref[:, vs]
            rg = rb_ref[0, rows, vs].astype(F32)
            o = o * (rg / (1.0 + jnp.exp(-rg)))
            out_ref[0, rows, vs] = o.astype(out_ref.dtype)
    for hd in range(H_B):
        st_ref[hd] = state[hd]


def _gla_call(qb, kb, vb, glr, rb, wg, bg, gn, bsz, s_len):
    ts = TS_GLA
    tok = lambda width: pl.BlockSpec((1, ts, width), lambda b, i: (b, i, 0))
    const = lambda shape: pl.BlockSpec(shape, lambda b, i: (0,) * len(shape))
    return pl.pallas_call(
        _gla_kernel,
        grid=(bsz, s_len // ts),
        in_specs=[tok(H_B * DK_B), tok(H_B * DK_B), tok(H_B * DV_B), tok(LANES), tok(H_B * DV_B),
                  const((LANES, H_B * DK_B)), const((1, H_B * DK_B)), const((1, H_B * DV_B))],
        out_specs=tok(H_B * DV_B),
        out_shape=jax.ShapeDtypeStruct((bsz, s_len, H_B * DV_B), MXU_DTYPE),
        scratch_shapes=[pltpu.VMEM((H_B, DV_B, DK_B), F32)],
        compiler_params=pltpu.CompilerParams(
            dimension_semantics=("parallel", "arbitrary"), vmem_limit_bytes=VMEM_LIMIT),
        name="gla_mixer",
    )(qb, kb, vb, glr, rb, wg, bg, gn)


SWA_SUB = WINDOW
SWA_WIN = SWA_SUB + WINDOW


def _swa_window_start(tile_start, sub):
    return jnp.maximum(tile_start + sub * SWA_SUB - WINDOW, 0)


def _swa_kernel(sink_ref, qc_ref, kc_ref, vc_ref, mb_ref, out_ref):
    i = pl.program_id(1)
    grp = H_C // KV_C
    for sub in range(TQ_SWA // SWA_SUB):
        qrows = slice(sub * SWA_SUB, (sub + 1) * SWA_SUB)
        start = pl.multiple_of(_swa_window_start(i * TQ_SWA, sub), WINDOW)
        k = kc_ref[0, pl.ds(start, SWA_WIN), :]
        v = vc_ref[0, pl.ds(start, SWA_WIN), :]
        outs = []
        for hd in range(H_C):
            kv = hd // grp
            q = qc_ref[0, qrows, hd * HEAD_DIM:(hd + 1) * HEAD_DIM]
            lg = (_dot_nt(q, k[:, kv * HEAD_DIM:(kv + 1) * HEAD_DIM]) * (HEAD_DIM ** -0.5)
                  + mb_ref[0, hd, sub])
            sink = sink_ref[hd]
            m = jnp.maximum(jnp.max(lg, axis=-1, keepdims=True), sink)
            e = jnp.exp(lg - m)
            p = e / (jnp.sum(e, axis=-1, keepdims=True) + jnp.exp(sink - m))
            outs.append(_dot(p.astype(MXU_DTYPE), v[:, kv * HEAD_DIM:(kv + 1) * HEAD_DIM]))
        out_ref[0, qrows, :] = jnp.concatenate(outs, axis=-1).astype(out_ref.dtype)


def _swa_call(sinks, qc, kc, vc, maskbias, bsz, s_len):
    tq = TQ_SWA
    return pl.pallas_call(
        _swa_kernel,
        grid=(bsz, s_len // tq),
        in_specs=[
            pl.BlockSpec(memory_space=pltpu.SMEM),
            pl.BlockSpec((1, tq, H_C * HEAD_DIM), lambda b, i: (b, i, 0)),
            pl.BlockSpec((1, s_len, KV_C * HEAD_DIM), lambda b, i: (b, 0, 0)),
            pl.BlockSpec((1, s_len, KV_C * HEAD_DIM), lambda b, i: (b, 0, 0)),
            pl.BlockSpec((1, H_C, tq // SWA_SUB, SWA_SUB, SWA_WIN),
                         lambda b, i: (jnp.minimum(i, 1), 0, 0, 0, 0)),
        ],
        out_specs=pl.BlockSpec((1, tq, H_C * HEAD_DIM), lambda b, i: (b, i, 0)),
        out_shape=jax.ShapeDtypeStruct((bsz, s_len, H_C * HEAD_DIM), MXU_DTYPE),
        compiler_params=pltpu.CompilerParams(
            dimension_semantics=("parallel", "arbitrary"), vmem_limit_bytes=VMEM_LIMIT),
        name="swa_mixer",
    )(sinks, qc, kc, vc, maskbias)


def _ffn_kernel(x_ref, oa_ref, ob_ref, oc_ref, gf_ref, gfin_ref,
                wo_hbm, wg_hbm, wu_hbm, wd_hbm, out_ref,
                wo_ref, wg_ref, wu_ref, wd_ref, sems, *, final_norm):
    @pl.when(pl.program_id(0) == 0)
    def _():
        pairs = ((wo_hbm, wo_ref), (wg_hbm, wg_ref), (wu_hbm, wu_ref), (wd_hbm, wd_ref))
        copies = [pltpu.make_async_copy(src, dst, sems.at[k]) for k, (src, dst) in enumerate(pairs)]
        for cp in copies:
            cp.start()
        for cp in copies:
            cp.wait()

    na = H_A * HEAD_DIM
    nb = H_B * DV_B
    x1 = (x_ref[...] + _dot(oa_ref[...], wo_ref[0:na, :])
          + _dot(ob_ref[...], wo_ref[na:na + nb, :])
          + _dot(oc_ref[...], wo_ref[na + nb:, :]))
    h = _rms(x1, gf_ref[...]).astype(MXU_DTYPE)
    y = x1
    for j in range(D_FF // TF_FFN):
        cols = slice(j * TF_FFN, (j + 1) * TF_FFN)
        gate = _dot(h, wg_ref[:, cols])
        up = _dot(h, wu_ref[:, cols])
        a = (gate / (1.0 + jnp.exp(-gate))) * up
        y = y + _dot(a.astype(MXU_DTYPE), wd_ref[cols, :])
    if final_norm:
        y = _rms(y, gfin_ref[...])
    out_ref[...] = y


def _ffn_call(x2, oa, ob, oc, wo, gf, wg, wu, wd, gfin, final_norm):
    t_tok = x2.shape[0]
    tm = TM_FFN
    tok = lambda width: pl.BlockSpec((tm, width), lambda i: (i, 0))
    const = lambda shape: pl.BlockSpec(shape, lambda i: (0,) * len(shape))
    in_hbm = pl.BlockSpec(memory_space=pl.ANY)
    return pl.pallas_call(
        functools.partial(_ffn_kernel, final_norm=final_norm),
        grid=(t_tok // tm,),
        in_specs=[tok(D_MODEL), tok(H_A * HEAD_DIM), tok(H_B * DV_B), tok(H_C * HEAD_DIM),
                  const((1, D_MODEL)), const((1, D_MODEL)), in_hbm, in_hbm, in_hbm, in_hbm],
        out_specs=tok(D_MODEL),
        out_shape=jax.ShapeDtypeStruct((t_tok, D_MODEL), F32),
        scratch_shapes=[pltpu.VMEM((D_MODEL, D_MODEL), MXU_DTYPE),
                        pltpu.VMEM((D_MODEL, D_FF), MXU_DTYPE),
                        pltpu.VMEM((D_MODEL, D_FF), MXU_DTYPE),
                        pltpu.VMEM((D_FF, D_MODEL), MXU_DTYPE),
                        pltpu.SemaphoreType.DMA((4,))],
        compiler_params=pltpu.CompilerParams(
            dimension_semantics=("arbitrary",), vmem_limit_bytes=VMEM_LIMIT),
        name="out_proj_ffn",
    )(x2, oa, ob, oc, gf, gfin, wo, wg, wu, wd)


def _rel_bucket(rel):
    nb = NUM_BUCKETS // 2
    max_exact = nb // 2
    n = jnp.abs(rel)
    side = jnp.where(rel > 0, nb, 0)
    nf = jnp.maximum(n, 1).astype(jnp.float32)
    large = max_exact + (jnp.log(nf / max_exact) / math.log(MAX_DISTANCE / max_exact)
                         * (nb - max_exact)).astype(jnp.int32)
    large = jnp.minimum(large, nb - 1)
    return side + jnp.where(n < max_exact, n, large)


def _bucket_ids(ndim, axis):
    shape = [1] * ndim
    shape[axis] = NUM_BUCKETS
    return jnp.arange(NUM_BUCKETS, dtype=jnp.int32).reshape(shape)


def _dsa_bias_tables(rel_bias):
    table = rel_bias[:, :H_A].astype(F32)
    far = _rel_bucket(jnp.full((), -2 * QBLK, jnp.int32))
    table = table - jnp.sum(jnp.where(_bucket_ids(2, 0) == far, table, 0.0), axis=0, keepdims=True)
    kk = jnp.arange(QBLK, dtype=jnp.int32)[:, None]
    qq = jnp.arange(QBLK, dtype=jnp.int32)[None, :]
    near = []
    for shift in (0, QBLK):
        bucket = _rel_bucket(kk - shift - qq)[:, None, None, :]
        b = jnp.sum(jnp.where(bucket == _bucket_ids(4, 1), table[None, :, :, None], 0.0), axis=1)
        near.append(b.reshape(QBLK, H_A * QBLK))
    near.append(jnp.zeros((QBLK, H_A * QBLK), F32))
    return jnp.stack(near)


def _swa_maskbias(rel_bias):
    table = rel_bias[:, H_A:].astype(F32)
    out = []
    for tile_start in (0, TQ_SWA):
        subs = []
        for sub in range(TQ_SWA // SWA_SUB):
            qq = tile_start + sub * SWA_SUB + jnp.arange(SWA_SUB, dtype=jnp.int32)[:, None]
            kk = (max(tile_start + sub * SWA_SUB - WINDOW, 0)
                  + jnp.arange(SWA_WIN, dtype=jnp.int32)[None, :])
            dc = qq // CHUNK - kk // CHUNK
            band = (dc >= 0) & (dc <= WIN_CHUNKS)
            bucket = _rel_bucket(kk - qq)[None, None]
            b = jnp.sum(jnp.where(bucket == _bucket_ids(4, 0), table[:, :, None, None], 0.0), axis=0)
            subs.append(jnp.where(band[None], b, NEG_INF))
        out.append(jnp.stack(subs, axis=1))
    return jnp.stack(out)


def kernel(x, w_in, w_out, norm_mix, norm_ffn, kv_norm, w_uk, w_uv, w_gate2, b_gate,
           gla_norm, sinks, rel_bias, w_ffn_gate, w_ffn_up, w_ffn_down, final_norm):
    bsz, s_len, _ = x.shape
    t_tok = bsz * s_len
    wd = MXU_DTYPE
    biasn = _dsa_bias_tables(rel_bias)
    maskbias = _swa_maskbias(rel_bias)
    x2 = x.reshape(t_tok, D_MODEL)
    row = lambda v: v.reshape(1, -1).astype(F32)
    for l in range(DEPTH):
        w_windows = _split_w_in(w_in[l].astype(wd))
        wuk = jnp.transpose(w_uk[l], (1, 2, 0)).astype(wd)
        wuv = jnp.transpose(w_uv[l], (1, 0, 2)).astype(wd)
        (qlat, ckv, ckvt, qidx, kidx, widxt, qb, kb, vb, glr, rb, qc, kc, vc) = _proj_call(
            x2, row(norm_mix[l]), w_windows, wuk, row(kv_norm[l]), bsz, s_len)
        b3 = lambda a: a.reshape(bsz, s_len, a.shape[-1])
        out_a = _dsa_call(qidx, widxt, qlat, b3(kidx), b3(ckv), ckvt,
                          biasn, wuv, bsz, s_len)
        wg2 = jnp.zeros((LANES, H_B * DK_B), F32).at[:GATE_RANK].set(w_gate2[l])
        out_b = _gla_call(b3(qb), b3(kb), b3(vb), b3(glr), b3(rb), wg2,
                          row(b_gate[l]), row(gla_norm[l]), bsz, s_len)
        out_c = _swa_call(sinks[l].astype(F32), b3(qc), b3(kc), b3(vc), maskbias, bsz, s_len)
        f2 = lambda a: a.reshape(t_tok, a.shape[-1])
        x2 = _ffn_call(x2, f2(out_a), f2(out_b), f2(out_c), w_out[l].astype(wd),
                       row(norm_ffn[l]), w_ffn_gate[l].astype(wd), w_ffn_up[l].astype(wd),
                       w_ffn_down[l].astype(wd), row(final_norm), l == DEPTH - 1)
    return x2.reshape(bsz, s_len, D_MODEL)
```

```python
import functools
import math

import jax
import jax.numpy as jnp
from jax import lax
from jax.experimental import pallas as pl
from jax.experimental.pallas import tpu as pltpu

D_MODEL = 1024
DEPTH = 2
CHUNK = 64
HEAD_DIM = 64
H_A = 4
KV_RANK = 128
IDX_HEADS = 4
IDX_DIM = 64
TOPK_MAX = 256
H_B = 4
DK_B = 64
DV_B = 128
GATE_RANK = 16
GATE_TEMP = 16.0
H_C = 4
KV_C = 2
WINDOW = 128
WIN_CHUNKS = WINDOW // CHUNK
NUM_BUCKETS = 32
MAX_DISTANCE = 128
D_FF = ((8 * D_MODEL // 3 + 255) // 256) * 256
EPS = 1e-6

IN_SPLITS = (H_A * HEAD_DIM, KV_RANK, IDX_HEADS * IDX_DIM, IDX_DIM, IDX_HEADS,
             H_B * DK_B, H_B * DK_B, H_B * DV_B, GATE_RANK, H_B * DV_B,
             H_C * HEAD_DIM, KV_C * HEAD_DIM, KV_C * HEAD_DIM)

LANES = 128
SUBLANES = 8
VMEM_LIMIT = 56 * 1024 * 1024

MXU_DTYPE = jnp.bfloat16
HALF_DTYPE = jnp.int16
F32 = jnp.float32
NEG_INF = float("-inf")
MASKED = -1e30
F32_MAX = float(jnp.finfo(jnp.float32).max)
INT_MIN = -2 ** 31
KEY_NEG_INF = INT_MIN + 2 ** 23
KEY_MIN_NORMAL = 2 ** 23
HALF_MIN = -2 ** 15

QBLK = 256
KV_EXT = KV_RANK + 16
ROW_CHUNK = 32
TM_PROJ = 512
TS_GLA = 256
TQ_SWA = 256
TM_FFN = 512
TF_FFN = D_FF


def _dot(a, b):
    return jnp.dot(a, b, preferred_element_type=F32)


def _dot_nt(a, b):
    return lax.dot_general(a, b, (((1,), (1,)), ((), ())), preferred_element_type=F32)


def _dot_tn(a, b):
    return lax.dot_general(a, b, (((0,), (0,)), ((), ())), preferred_element_type=F32)


def _rms(x, g):
    return x * lax.rsqrt(jnp.mean(x * x, axis=-1, keepdims=True) + EPS) * g


_W_STARTS = (0, sum(IN_SPLITS[:5]), sum(IN_SPLITS[:9]))
_W_WIDTHS = (768, 1152, 1024)
_C_QA = (0, 0)
_C_CKV = (0, 256)
_C_QIDX = (0, 384)
_C_KW = (0, 640)
_C_QB = (1, 0)
_C_KB = (1, 256)
_C_VB = (1, 512)
_C_GLR = (1, 1024)
_C_RB = (2, 0)
_C_QC = (2, 512)
_C_KC = (2, 768)
_C_VC = (2, 896)


def _proj_kernel(x_ref, g_ref, wa_ref, wb_ref, wc_ref, wuk_ref, kvn_ref,
                 qlat_ref, ckv_ref, ckvt_ref, qidx_ref, kidx_ref, widxt_ref,
                 qb_ref, kb_ref, vb_ref, glr_ref, rb_ref, qc_ref, kc_ref, vc_ref):
    h = _rms(x_ref[...], g_ref[...]).astype(MXU_DTYPE)

    p = [_dot(h, w_ref[...]) for w_ref in (wa_ref, wb_ref, wc_ref)]

    def seg(where, width):
        window, c0 = where
        return p[window][:, c0:c0 + width]

    qa = seg(_C_QA, 256)
    for hd in range(H_A):
        qh = qa[:, hd * HEAD_DIM:(hd + 1) * HEAD_DIM].astype(MXU_DTYPE)
        ql = _dot(qh, wuk_ref[hd]) * (HEAD_DIM ** -0.5)
        for t in range(TM_PROJ // QBLK):
            qlat_ref[0, t, :, hd * QBLK:(hd + 1) * QBLK] = (
                ql[t * QBLK:(t + 1) * QBLK, :].T.astype(qlat_ref.dtype))

    ckv = _rms(seg(_C_CKV, KV_RANK), kvn_ref[...])
    ckv_ref[...] = ckv.astype(ckv_ref.dtype)
    for t in range(TM_PROJ // QBLK):
        ckvt_ref[0, t, 0:KV_RANK, :] = ckv[t * QBLK:(t + 1) * QBLK, :].T.astype(ckvt_ref.dtype)
        ckvt_ref[0, t, KV_RANK:, :] = jnp.ones((KV_EXT - KV_RANK, QBLK), ckvt_ref.dtype)

    qi = seg(_C_QIDX, 256)
    for t in range(TM_PROJ // QBLK):
        qit = qi[t * QBLK:(t + 1) * QBLK, :].T
        for hd in range(IDX_HEADS):
            qidx_ref[0, t, :, hd * QBLK:(hd + 1) * QBLK] = (
                qit[hd * IDX_DIM:(hd + 1) * IDX_DIM, :].astype(qidx_ref.dtype))
    kw = seg(_C_KW, LANES)
    kidx_ref[...] = kw[:, :IDX_DIM].astype(kidx_ref.dtype)
    kwt = kw.T
    widxt_ref[0] = kwt[IDX_DIM:IDX_DIM + SUBLANES, :] * (IDX_HEADS ** -0.5 * IDX_DIM ** -0.5)

    qb_ref[...] = seg(_C_QB, 256).astype(qb_ref.dtype)
    kb_ref[...] = seg(_C_KB, 256).astype(kb_ref.dtype)
    vb_ref[...] = seg(_C_VB, 512).astype(vb_ref.dtype)
    glr_ref[...] = seg(_C_GLR, LANES)
    rb_ref[...] = seg(_C_RB, 512).astype(rb_ref.dtype)
    qc_ref[...] = seg(_C_QC, 256).astype(qc_ref.dtype)
    kc_ref[...] = seg(_C_KC, LANES).astype(kc_ref.dtype)
    vc_ref[...] = seg(_C_VC, LANES).astype(vc_ref.dtype)


def _split_w_in(w):
    return tuple(w[:, c0:c0 + n] for c0, n in zip(_W_STARTS, _W_WIDTHS))


def _proj_call(x2, g, w_windows, wuk, kvn, bsz, s_len):
    t_tok = x2.shape[0]
    tm = TM_PROJ
    n_s = s_len // tm
    act = MXU_DTYPE

    def tok(width):
        return pl.BlockSpec((tm, width), lambda i: (i, 0))

    def const(shape):
        return pl.BlockSpec(shape, lambda i: (0,) * len(shape))

    def qblocks(depth):
        return pl.BlockSpec((1, tm // QBLK, depth, H_A * QBLK), lambda i: (i // n_s, i % n_s, 0, 0))

    out_shape = [
        jax.ShapeDtypeStruct((bsz, s_len // QBLK, KV_RANK, H_A * QBLK), act),
        jax.ShapeDtypeStruct((t_tok, KV_RANK), act),
        jax.ShapeDtypeStruct((bsz, s_len // QBLK, KV_EXT, QBLK), act),
        jax.ShapeDtypeStruct((bsz, s_len // QBLK, IDX_DIM, IDX_HEADS * QBLK), act),
        jax.ShapeDtypeStruct((t_tok, IDX_DIM), act),
        jax.ShapeDtypeStruct((bsz, SUBLANES, s_len), F32),
        jax.ShapeDtypeStruct((t_tok, H_B * DK_B), act),
        jax.ShapeDtypeStruct((t_tok, H_B * DK_B), act),
        jax.ShapeDtypeStruct((t_tok, H_B * DV_B), act),
        jax.ShapeDtypeStruct((t_tok, LANES), F32),
        jax.ShapeDtypeStruct((t_tok, H_B * DV_B), act),
        jax.ShapeDtypeStruct((t_tok, H_C * HEAD_DIM), act),
        jax.ShapeDtypeStruct((t_tok, KV_C * HEAD_DIM), act),
        jax.ShapeDtypeStruct((t_tok, KV_C * HEAD_DIM), act),
    ]
    out_specs = [
        qblocks(KV_RANK), tok(KV_RANK),
        pl.BlockSpec((1, tm // QBLK, KV_EXT, QBLK), lambda i: (i // n_s, i % n_s, 0, 0)),
        qblocks(IDX_DIM), tok(IDX_DIM),
        pl.BlockSpec((1, SUBLANES, tm), lambda i: (i // n_s, 0, i % n_s)),
        tok(H_B * DK_B), tok(H_B * DK_B), tok(H_B * DV_B), tok(LANES), tok(H_B * DV_B),
        tok(H_C * HEAD_DIM), tok(KV_C * HEAD_DIM), tok(KV_C * HEAD_DIM),
    ]
    return pl.pallas_call(
        _proj_kernel,
        grid=(t_tok // tm,),
        in_specs=[tok(D_MODEL), const((1, D_MODEL))]
                 + [const((D_MODEL, n)) for n in _W_WIDTHS] + [
                  const((H_A, HEAD_DIM, KV_RANK)), const((1, KV_RANK))],
        out_specs=out_specs,
        out_shape=out_shape,
        compiler_params=pltpu.CompilerParams(
            dimension_semantics=("parallel",), vmem_limit_bytes=VMEM_LIMIT),
        name="norm_in_proj",
    )(x2, g, *w_windows, wuk, kvn)


def _float_to_key(x):
    bits = lax.bitcast_convert_type(x, jnp.int32)
    return jnp.where(bits >= 0, bits, INT_MIN - bits)


def _key_to_float(k):
    k = jnp.where((k > 0) & (k < KEY_MIN_NORMAL), KEY_MIN_NORMAL, k)
    k = jnp.where((k < 0) & (k > -KEY_MIN_NORMAL), 0, k)
    bits = jnp.where(k >= 0, k, INT_MIN - k)
    return lax.bitcast_convert_type(bits, F32)


def _dsa_kernel(qidx_ref, widxt_ref, qlat_ref, kidx_ref, ckv_ref, ckvt_ref,
                biasn_ref, wuv_ref, out_ref,
                s_ref, hi_ref, lo_ref, lg_ref, acc_ref, da_ref, db_ref, pa_ref, pb_ref,
                da16_ref, *, s_len, topk):
    jq = pl.program_id(1)
    n_tiles = jq + 1
    n_virtual = s_len - n_tiles * QBLK
    hq = H_A * QBLK
    head_cols = [slice(hd * QBLK, (hd + 1) * QBLK) for hd in range(H_A)]

    def rows(j, r0=0, n=QBLK):
        return pl.ds(pl.multiple_of(j * QBLK + r0, n), n)

    def fold(x, op):
        return op(x.reshape(x.shape[0] // SUBLANES, SUBLANES, x.shape[-1]), axis=0)

    def pipeline(n, produce, consume, carry, bufs):
        a, b = bufs
        produce(jnp.maximum(n - 1, 0), a)

        def pair(i, carry):
            j = n - 1 - 2 * i
            produce(j - 1, b)
            carry = consume(j, a, carry)
            produce(jnp.maximum(j - 2, 0), a)
            return consume(j - 1, b, carry)

        carry = lax.fori_loop(0, n // 2, pair, carry)
        return lax.cond(n % 2 == 1, lambda c: consume(0, a, c), lambda c: c, carry)

    qis = qidx_ref[0, 0]
    wt = widxt_ref[0]

    def stage_scores(j, buf):
        buf[...] = _dot(kidx_ref[0, rows(j), :], qis)

    def put_scores(j, buf, carry, diagonal=False):
        for r0 in range(0, QBLK, ROW_CHUNK):
            rc = slice(r0, r0 + ROW_CHUNK)
            s = jnp.maximum(buf[rc, head_cols[0]], 0.0) * wt[0:1, :]
            for hd in range(1, IDX_HEADS):
                s = s + jnp.maximum(buf[rc, head_cols[hd]], 0.0) * wt[hd:hd + 1, :]
            if diagonal:
                kchunk = (r0 + lax.broadcasted_iota(jnp.int32, (ROW_CHUNK, 1), 0)) // CHUNK
                qchunk = lax.broadcasted_iota(jnp.int32, (1, QBLK), 1) // CHUNK
                s = jnp.where(kchunk <= qchunk, s, NEG_INF)
            s_ref[rows(j, r0, ROW_CHUNK), :] = s
            key = _float_to_key(s)
            hi_ref[rows(j, r0, ROW_CHUNK), :] = jnp.right_shift(key, 16).astype(HALF_DTYPE)
            lo_ref[rows(j, r0, ROW_CHUNK), :] = (jnp.bitwise_and(key, 0xFFFF) + HALF_MIN).astype(HALF_DTYPE)
        return carry

    stage_scores(jq, da_ref)
    put_scores(jq, da_ref, 0, diagonal=True)
    pipeline(jq, stage_scores, put_scores, 0, (da_ref, db_ref))

    def count_tiles(tile_count, zero):
        def quad(i, acc):
            j = 4 * i
            return acc + ((tile_count(j) + tile_count(j + 1)) + (tile_count(j + 2) + tile_count(j + 3)))

        acc = lax.fori_loop(0, n_tiles // 4, quad, zero)
        return lax.fori_loop(4 * (n_tiles // 4), n_tiles, lambda j, acc: acc + tile_count(j), acc)

    packed = 2 * SUBLANES

    def count_half(ref, t, strict=False):
        th = t.astype(HALF_DTYPE)
        one, zero = jnp.ones((), HALF_DTYPE), jnp.zeros((), HALF_DTYPE)

        def tile_count(j):
            x = ref[rows(j), :]
            hit = jnp.where(x > th if strict else x >= th, one, zero)
            slabs = [hit[r0:r0 + packed, :] for r0 in range(0, QBLK, packed)]
            while len(slabs) > 1:
                slabs = [a + b for a, b in zip(slabs[0::2], slabs[1::2])]
            return slabs[0]

        acc = count_tiles(tile_count, jnp.zeros((packed, QBLK), HALF_DTYPE))
        return jnp.sum(acc.astype(jnp.int32), axis=0, keepdims=True)

    def bisect_half(count_at_least, target):
        def step(i, t):
            cand = t + jnp.left_shift(jnp.int32(1), 15 - i)
            return jnp.where(count_at_least(cand) >= target, cand, t)

        return lax.fori_loop(0, 16, step, jnp.full((1, QBLK), HALF_MIN, jnp.int32))

    hi_neg_inf = KEY_NEG_INF >> 16
    g_hi = bisect_half(
        lambda t: count_half(hi_ref, t) + jnp.where(t <= hi_neg_inf, n_virtual, 0), topk)
    above = count_half(hi_ref, g_hi, strict=True) + jnp.where(g_hi < hi_neg_inf, n_virtual, 0)

    def mark_candidates(j, carry):
        g = g_hi.astype(HALF_DTYPE)
        da16_ref[rows(j), :] = jnp.where(hi_ref[rows(j), :] == g, lo_ref[rows(j), :],
                                         jnp.full((), HALF_MIN, HALF_DTYPE))
        return carry

    lax.fori_loop(0, n_tiles, mark_candidates, 0)
    g_lo = bisect_half(lambda t: count_half(da16_ref, t), topk - above)
    t_key = g_hi * 65536 + (g_lo - HALF_MIN)
    tf = _key_to_float(t_key)

    def count_ge(tf):
        acc = count_tiles(lambda j: fold(jnp.where(s_ref[rows(j), :] >= tf, 1, 0), jnp.sum),
                          jnp.zeros((SUBLANES, QBLK), jnp.int32))
        return jnp.sum(acc, axis=0, keepdims=True) + jnp.where(NEG_INF >= tf, n_virtual, 0)

    cnt_t = count_ge(tf)

    has_tie = jnp.max(jnp.where((cnt_t == topk) | (tf == NEG_INF), 0, 1))

    @pl.when(has_tie > 0)
    def _():
        t_next = _key_to_float(t_key + 1)
        need = (topk - count_ge(t_next)).astype(F32)
        r = lax.broadcasted_iota(jnp.int32, (QBLK, QBLK), 0)
        c = lax.broadcasted_iota(jnp.int32, (QBLK, QBLK), 1)
        tril = (c <= r).astype(MXU_DTYPE)

        def body(j, carry):
            tile = s_ref[rows(j), :]
            eq = (tile >= tf) & jnp.logical_not(tile >= t_next)
            pre = _dot(tril, eq.astype(MXU_DTYPE)) + carry
            s_ref[rows(j), :] = jnp.where(eq & (pre > need), NEG_INF, tile)
            return pre[QBLK - 1:QBLK, :]

        lax.fori_loop(0, n_tiles, body, jnp.zeros((1, QBLK), F32))

    t_sel = jnp.maximum(tf, -F32_MAX)
    def logits_tile(j, mx):
        ck = ckv_ref[0, rows(j), :]
        near = jnp.minimum(jq - j, 2)
        parts = []
        for hd in range(H_A):
            lg = _dot(ck, qlat_ref[0, 0, :, head_cols[hd]])
            pm = mx[:, head_cols[hd]]
            for r0 in range(0, QBLK, ROW_CHUNK):
                rc = slice(r0, r0 + ROW_CHUNK)
                sel = s_ref[rows(j, r0, ROW_CHUNK), :] >= t_sel
                v = jnp.where(sel, lg[rc, :] + biasn_ref[near, rc, head_cols[hd]], MASKED)
                lg_ref[rows(j, r0, ROW_CHUNK), head_cols[hd]] = v
                pm = jnp.maximum(pm, fold(v, jnp.max))
            parts.append(pm)
        return jnp.concatenate(parts, axis=-1)

    mx = lax.fori_loop(0, n_tiles // 2,
                       lambda i, mx: logits_tile(2 * i + 1, logits_tile(2 * i, mx)),
                       jnp.full((SUBLANES, hq), MASKED, F32))
    mx = lax.cond(n_tiles % 2 == 1, lambda mx: logits_tile(jq, mx), lambda mx: mx, mx)
    m = jnp.max(mx, axis=0, keepdims=True)

    acc_ref[...] = jnp.zeros(acc_ref.shape, F32)

    def stage_probs(j, buf):
        for r0 in range(0, QBLK, ROW_CHUNK):
            p = jnp.exp(lg_ref[rows(j, r0, ROW_CHUNK), :] - m)
            buf[r0:r0 + ROW_CHUNK, :] = p.astype(buf.dtype)

    def accumulate(j, buf, carry):
        acc_ref[...] += _dot(ckvt_ref[0, j], buf[...])
        return carry

    pipeline(n_tiles, stage_probs, accumulate, 0, (pa_ref, pb_ref))

    l = acc_ref[KV_RANK:KV_RANK + 1, :]
    outs = []
    for hd in range(H_A):
        o_t = acc_ref[0:KV_RANK, head_cols[hd]] / l[:, head_cols[hd]]
        outs.append(_dot(o_t.T.astype(MXU_DTYPE), wuv_ref[hd]))
    out_ref[0] = jnp.concatenate(outs, axis=-1).astype(out_ref.dtype)


def _dsa_call(qidx, widxt, qlat, kidx, ckv, ckvt, biasn, wuv, bsz, s_len):
    topk = min(TOPK_MAX, s_len // 4)
    nblk = s_len // QBLK
    qblock_t = lambda depth: pl.BlockSpec((1, 1, depth, H_A * QBLK), lambda b, i: (b, i, 0, 0))
    const = lambda shape: pl.BlockSpec(shape, lambda b, i: (0,) * len(shape))
    return pl.pallas_call(
        functools.partial(_dsa_kernel, s_len=s_len, topk=topk),
        grid=(bsz, nblk),
        in_specs=[
            qblock_t(IDX_DIM),
            pl.BlockSpec((1, SUBLANES, QBLK), lambda b, i: (b, 0, i)),
            qblock_t(KV_RANK),
            pl.BlockSpec((1, s_len, IDX_DIM), lambda b, i: (b, 0, 0)),
            pl.BlockSpec((1, s_len, KV_RANK), lambda b, i: (b, 0, 0)),
            pl.BlockSpec((1, nblk, KV_EXT, QBLK), lambda b, i: (b, 0, 0, 0)),
            const((3, QBLK, H_A * QBLK)),
            const((H_A, KV_RANK, HEAD_DIM)),
        ],
        out_specs=pl.BlockSpec((1, QBLK, H_A * HEAD_DIM), lambda b, i: (b, i, 0)),
        out_shape=jax.ShapeDtypeStruct((bsz, s_len, H_A * HEAD_DIM), MXU_DTYPE),
        scratch_shapes=[
            pltpu.VMEM((s_len, QBLK), F32),
            pltpu.VMEM((s_len, QBLK), HALF_DTYPE),
            pltpu.VMEM((s_len, QBLK), HALF_DTYPE),
            pltpu.VMEM((s_len, H_A * QBLK), F32),
            pltpu.VMEM((KV_EXT, H_A * QBLK), F32),
            pltpu.VMEM((QBLK, H_A * QBLK), F32),
            pltpu.VMEM((QBLK, H_A * QBLK), F32),
            pltpu.VMEM((QBLK, H_A * QBLK), MXU_DTYPE),
            pltpu.VMEM((QBLK, H_A * QBLK), MXU_DTYPE),
            pltpu.VMEM((s_len, QBLK), HALF_DTYPE),
        ],
        compiler_params=pltpu.CompilerParams(
            dimension_semantics=("parallel", "arbitrary"), vmem_limit_bytes=VMEM_LIMIT),
        name="dsa_mixer",
    )(qidx, widxt, qlat, kidx, ckv, ckvt, biasn, wuv)


def _split3(x):
    a = x.astype(MXU_DTYPE)
    r = x - a.astype(F32)
    b = r.astype(MXU_DTYPE)
    c = (r - b.astype(F32)).astype(MXU_DTYPE)
    return a, b, c


def _gla_kernel(qb_ref, kb_ref, vb_ref, glr_ref, rb_ref, wg_ref, bg_ref, gn_ref,
                out_ref, st_ref):
    n_chunks = TS_GLA // CHUNK
    assert n_chunks * DK_B == TS_GLA

    @pl.when(pl.program_id(1) == 0)
    def _():
        st_ref[...] = jnp.zeros(st_ref.shape, F32)

    g_hi, g_lo, _ = _split3(glr_ref[0])
    w_hi, w_lo, _ = _split3(wg_ref[...])
    z = _dot(g_hi, w_hi) + (_dot(g_hi, w_lo) + _dot(g_lo, w_hi)) + bg_ref[...]
    g = (jnp.minimum(z, 0.0) - jnp.log1p(jnp.exp(-jnp.abs(z)))) / GATE_TEMP

    r = lax.broadcasted_iota(jnp.int32, (TS_GLA, TS_GLA), 0)
    c = lax.broadcasted_iota(jnp.int32, (TS_GLA, TS_GLA), 1)
    same_chunk = (r // CHUNK) == (c // CHUNK)
    causal = same_chunk & (c <= r)
    in_block = (r // CHUNK) == (c // DK_B)

    g1, g2, g3 = _split3(g)
    tril = causal.astype(MXU_DTYPE)
    ones = same_chunk.astype(MXU_DTYPE)
    b = _dot(tril, g1) + (_dot(tril, g2) + _dot(tril, g3))
    b_last = _dot(ones, g1) + (_dot(ones, g2) + _dot(ones, g3))
    q = qb_ref[0].astype(F32) * (DK_B ** -0.5)
    k = kb_ref[0].astype(F32)
    qe = q * jnp.exp(b)
    ke = (k * jnp.exp(-b)).astype(MXU_DTYPE)
    kd = k * jnp.exp(b_last - b)
    decay = jnp.exp(b_last)

    def expand(x):
        return jnp.where(in_block, jnp.concatenate([x] * n_chunks, axis=1), 0.0).astype(MXU_DTYPE)

    for hd in range(H_B):
        ks = slice(hd * DK_B, (hd + 1) * DK_B)
        vs = slice(hd * DV_B, (hd + 1) * DV_B)
        v = vb_ref[0, :, vs]
        qe_h = qe[:, ks]
        att = jnp.where(causal, _dot_nt(qe_h.astype(MXU_DTYPE), ke[:, ks]), 0.0)
        kv_t = _dot_tn(v, expand(kd[:, ks]))
        st = st_ref[hd]
        starts = []
        for ci in range(n_chunks):
            starts.append(st)
            st = (st * decay[ci * CHUNK:ci * CHUNK + 1, ks]
                  + kv_t[:, ci * DK_B:(ci + 1) * DK_B])
        st_ref[hd] = st
        st_all = jnp.concatenate(starts, axis=1).astype(MXU_DTYPE)
        o = _dot(att.astype(MXU_DTYPE), v) + _dot_nt(expand(qe_h), st_all)
        o = o * lax.rsqrt(jnp.mean(o * o, axis=-1, keepdims=True) + EPS) * gn_ref[:, vs]
        rg = rb_ref[0, :, vs].astype(F32)
        o = o * (rg / (1.0 + jnp.exp(-rg)))
        out_ref[0, :, vs] = o.astype(out_ref.dtype)


def _gla_call(qb, kb, vb, glr, rb, wg, bg, gn, bsz, s_len):
    ts = TS_GLA
    tok = lambda width: pl.BlockSpec((1, ts, width), lambda b, i: (b, i, 0))
    const = lambda shape: pl.BlockSpec(shape, lambda b, i: (0,) * len(shape))
    return pl.pallas_call(
        _gla_kernel,
        grid=(bsz, s_len // ts),
        in_specs=[tok(H_B * DK_B), tok(H_B * DK_B), tok(H_B * DV_B), tok(LANES), tok(H_B * DV_B),
                  const((LANES, H_B * DK_B)), const((1, H_B * DK_B)), const((1, H_B * DV_B))],
        out_specs=tok(H_B * DV_B),
        out_shape=jax.ShapeDtypeStruct((bsz, s_len, H_B * DV_B), MXU_DTYPE),
        scratch_shapes=[pltpu.VMEM((H_B, DV_B, DK_B), F32)],
        compiler_params=pltpu.CompilerParams(
            dimension_semantics=("parallel", "arbitrary"), vmem_limit_bytes=VMEM_LIMIT),
        name="gla_mixer",
    )(qb, kb, vb, glr, rb, wg, bg, gn)


SWA_SUB = WINDOW
SWA_WIN = SWA_SUB + WINDOW


def _swa_window_start(tile_start, sub):
    return jnp.maximum(tile_start + sub * SWA_SUB - WINDOW, 0)


def _swa_kernel(sink_ref, qc_ref, kc_ref, vc_ref, mb_ref, out_ref):
    i = pl.program_id(1)
    grp = H_C // KV_C
    for sub in range(TQ_SWA // SWA_SUB):
        qrows = slice(sub * SWA_SUB, (sub + 1) * SWA_SUB)
        start = pl.multiple_of(_swa_window_start(i * TQ_SWA, sub), WINDOW)
        k = kc_ref[0, pl.ds(start, SWA_WIN), :]
        v = vc_ref[0, pl.ds(start, SWA_WIN), :]
        outs = []
        for hd in range(H_C):
            kv = hd // grp
            q = qc_ref[0, qrows, hd * HEAD_DIM:(hd + 1) * HEAD_DIM]
            lg = (_dot_nt(q, k[:, kv * HEAD_DIM:(kv + 1) * HEAD_DIM]) * (HEAD_DIM ** -0.5)
                  + mb_ref[0, hd, sub])
            sink = sink_ref[hd]
            m = jnp.maximum(jnp.max(lg, axis=-1, keepdims=True), sink)
            e = jnp.exp(lg - m)
            p = e / (jnp.sum(e, axis=-1, keepdims=True) + jnp.exp(sink - m))
            outs.append(_dot(p.astype(MXU_DTYPE), v[:, kv * HEAD_DIM:(kv + 1) * HEAD_DIM]))
        out_ref[0, qrows, :] = jnp.concatenate(outs, axis=-1).astype(out_ref.dtype)


def _swa_call(sinks, qc, kc, vc, maskbias, bsz, s_len):
    tq = TQ_SWA
    return pl.pallas_call(
        _swa_kernel,
        grid=(bsz, s_len // tq),
        in_specs=[
            pl.BlockSpec(memory_space=pltpu.SMEM),
            pl.BlockSpec((1, tq, H_C * HEAD_DIM), lambda b, i: (b, i, 0)),
            pl.BlockSpec((1, s_len, KV_C * HEAD_DIM), lambda b, i: (b, 0, 0)),
            pl.BlockSpec((1, s_len, KV_C * HEAD_DIM), lambda b, i: (b, 0, 0)),
            pl.BlockSpec((1, H_C, tq // SWA_SUB, SWA_SUB, SWA_WIN),
                         lambda b, i: (jnp.minimum(i, 1), 0, 0, 0, 0)),
        ],
        out_specs=pl.BlockSpec((1, tq, H_C * HEAD_DIM), lambda b, i: (b, i, 0)),
        out_shape=jax.ShapeDtypeStruct((bsz, s_len, H_C * HEAD_DIM), MXU_DTYPE),
        compiler_params=pltpu.CompilerParams(
            dimension_semantics=("parallel", "arbitrary"), vmem_limit_bytes=VMEM_LIMIT),
        name="swa_mixer",
    )(sinks, qc, kc, vc, maskbias)


def _ffn_kernel(x_ref, oa_ref, ob_ref, oc_ref, gf_ref, gfin_ref,
                wo_hbm, wg_hbm, wu_hbm, wd_hbm, out_ref,
                wo_ref, wg_ref, wu_ref, wd_ref, sems, *, final_norm):
    @pl.when(pl.program_id(0) == 0)
    def _():
        pairs = ((wo_hbm, wo_ref), (wg_hbm, wg_ref), (wu_hbm, wu_ref), (wd_hbm, wd_ref))
        copies = [pltpu.make_async_copy(src, dst, sems.at[k]) for k, (src, dst) in enumerate(pairs)]
        for cp in copies:
            cp.start()
        for cp in copies:
            cp.wait()

    na = H_A * HEAD_DIM
    nb = H_B * DV_B
    x1 = (x_ref[...] + _dot(oa_ref[...], wo_ref[0:na, :])
          + _dot(ob_ref[...], wo_ref[na:na + nb, :])
          + _dot(oc_ref[...], wo_ref[na + nb:, :]))
    h = _rms(x1, gf_ref[...]).astype(MXU_DTYPE)
    y = x1
    for j in range(D_FF // TF_FFN):
        cols = slice(j * TF_FFN, (j + 1) * TF_FFN)
        gate = _dot(h, wg_ref[:, cols])
        up = _dot(h, wu_ref[:, cols])
        a = (gate / (1.0 + jnp.exp(-gate))) * up
        y = y + _dot(a.astype(MXU_DTYPE), wd_ref[cols, :])
    if final_norm:
        y = _rms(y, gfin_ref[...])
    out_ref[...] = y


def _ffn_call(x2, oa, ob, oc, wo, gf, wg, wu, wd, gfin, final_norm):
    t_tok = x2.shape[0]
    tm = TM_FFN
    tok = lambda width: pl.BlockSpec((tm, width), lambda i: (i, 0))
    const = lambda shape: pl.BlockSpec(shape, lambda i: (0,) * len(shape))
    in_hbm = pl.BlockSpec(memory_space=pl.ANY)
    return pl.pallas_call(
        functools.partial(_ffn_kernel, final_norm=final_norm),
        grid=(t_tok // tm,),
        in_specs=[tok(D_MODEL), tok(H_A * HEAD_DIM), tok(H_B * DV_B), tok(H_C * HEAD_DIM),
                  const((1, D_MODEL)), const((1, D_MODEL)), in_hbm, in_hbm, in_hbm, in_hbm],
        out_specs=tok(D_MODEL),
        out_shape=jax.ShapeDtypeStruct((t_tok, D_MODEL), F32),
        scratch_shapes=[pltpu.VMEM((D_MODEL, D_MODEL), MXU_DTYPE),
                        pltpu.VMEM((D_MODEL, D_FF), MXU_DTYPE),
                        pltpu.VMEM((D_MODEL, D_FF), MXU_DTYPE),
                        pltpu.VMEM((D_FF, D_MODEL), MXU_DTYPE),
                        pltpu.SemaphoreType.DMA((4,))],
        compiler_params=pltpu.CompilerParams(
            dimension_semantics=("arbitrary",), vmem_limit_bytes=VMEM_LIMIT),
        name="out_proj_ffn",
    )(x2, oa, ob, oc, gf, gfin, wo, wg, wu, wd)


def _rel_bucket(rel):
    nb = NUM_BUCKETS // 2
    max_exact = nb // 2
    n = jnp.abs(rel)
    side = jnp.where(rel > 0, nb, 0)
    nf = jnp.maximum(n, 1).astype(jnp.float32)
    large = max_exact + (jnp.log(nf / max_exact) / math.log(MAX_DISTANCE / max_exact)
                         * (nb - max_exact)).astype(jnp.int32)
    large = jnp.minimum(large, nb - 1)
    return side + jnp.where(n < max_exact, n, large)


def _bucket_ids(ndim, axis):
    shape = [1] * ndim
    shape[axis] = NUM_BUCKETS
    return jnp.arange(NUM_BUCKETS, dtype=jnp.int32).reshape(shape)


def _dsa_bias_tables(rel_bias):
    table = rel_bias[:, :H_A].astype(F32)
    far = _rel_bucket(jnp.full((), -2 * QBLK, jnp.int32))
    table = table - jnp.sum(jnp.where(_bucket_ids(2, 0) == far, table, 0.0), axis=0, keepdims=True)
    kk = jnp.arange(QBLK, dtype=jnp.int32)[:, None]
    qq = jnp.arange(QBLK, dtype=jnp.int32)[None, :]
    near = []
    for shift in (0, QBLK):
        bucket = _rel_bucket(kk - shift - qq)[:, None, None, :]
        b = jnp.sum(jnp.where(bucket == _bucket_ids(4, 1), table[None, :, :, None], 0.0), axis=1)
        near.append(b.reshape(QBLK, H_A * QBLK))
    near.append(jnp.zeros((QBLK, H_A * QBLK), F32))
    return jnp.stack(near)


def _swa_maskbias(rel_bias):
    table = rel_bias[:, H_A:].astype(F32)
    out = []
    for tile_start in (0, TQ_SWA):
        subs = []
        for sub in range(TQ_SWA // SWA_SUB):
            qq = tile_start + sub * SWA_SUB + jnp.arange(SWA_SUB, dtype=jnp.int32)[:, None]
            kk = (max(tile_start + sub * SWA_SUB - WINDOW, 0)
                  + jnp.arange(SWA_WIN, dtype=jnp.int32)[None, :])
            dc = qq // CHUNK - kk // CHUNK
            band = (dc >= 0) & (dc <= WIN_CHUNKS)
            bucket = _rel_bucket(kk - qq)[None, None]
            b = jnp.sum(jnp.where(bucket == _bucket_ids(4, 0), table[:, :, None, None], 0.0), axis=0)
            subs.append(jnp.where(band[None], b, NEG_INF))
        out.append(jnp.stack(subs, axis=1))
    return jnp.stack(out)


def kernel(x, w_in, w_out, norm_mix, norm_ffn, kv_norm, w_uk, w_uv, w_gate2, b_gate,
           gla_norm, sinks, rel_bias, w_ffn_gate, w_ffn_up, w_ffn_down, final_norm):
    bsz, s_len, _ = x.shape
    t_tok = bsz * s_len
    wd = MXU_DTYPE
    biasn = _dsa_bias_tables(rel_bias)
    maskbias = _swa_maskbias(rel_bias)
    x2 = x.reshape(t_tok, D_MODEL)
    row = lambda v: v.reshape(1, -1).astype(F32)
    for l in range(DEPTH):
        w_windows = _split_w_in(w_in[l].astype(wd))
        wuk = jnp.transpose(w_uk[l], (1, 2, 0)).astype(wd)
        wuv = jnp.transpose(w_uv[l], (1, 0, 2)).astype(wd)
        (qlat, ckv, ckvt, qidx, kidx, widxt, qb, kb, vb, glr, rb, qc, kc, vc) = _proj_call(
            x2, row(norm_mix[l]), w_windows, wuk, row(kv_norm[l]), bsz, s_len)
        b3 = lambda a: a.reshape(bsz, s_len, a.shape[-1])
        out_a = _dsa_call(qidx, widxt, qlat, b3(kidx), b3(ckv), ckvt,
                          biasn, wuv, bsz, s_len)
        wg2 = jnp.zeros((LANES, H_B * DK_B), F32).at[:GATE_RANK].set(w_gate2[l])
        out_b = _gla_call(b3(qb), b3(kb), b3(vb), b3(glr), b3(rb), wg2,
                          row(b_gate[l]), row(gla_norm[l]), bsz, s_len)
        out_c = _swa_call(sinks[l].astype(F32), b3(qc), b3(kc), b3(vc), maskbias, bsz, s_len)
        f2 = lambda a: a.reshape(t_tok, a.shape[-1])
        x2 = _ffn_call(x2, f2(out_a), f2(out_b), f2(out_c), w_out[l].astype(wd),
                       row(norm_ffn[l]), w_ffn_gate[l].astype(wd), w_ffn_up[l].astype(wd),
                       w_ffn_down[l].astype(wd), row(final_norm), l == DEPTH - 1)
    return x2.reshape(bsz, s_len, D_MODEL)
```

```python
import functools
import math

import jax
import jax.numpy as jnp
from jax import lax
from jax.experimental import pallas as pl
from jax.experimental.pallas import tpu as pltpu

D_MODEL = 1024
DEPTH = 2
CHUNK = 64
HEAD_DIM = 64
H_A = 4
KV_RANK = 128
IDX_HEADS = 4
IDX_DIM = 64
TOPK_MAX = 256
H_B = 4
DK_B = 64
DV_B = 128
GATE_RANK = 16
GATE_TEMP = 16.0
H_C = 4
KV_C = 2
WINDOW = 128
WIN_CHUNKS = WINDOW // CHUNK
NUM_BUCKETS = 32
MAX_DISTANCE = 128
D_FF = ((8 * D_MODEL // 3 + 255) // 256) * 256
EPS = 1e-6

IN_SPLITS = (H_A * HEAD_DIM, KV_RANK, IDX_HEADS * IDX_DIM, IDX_DIM, IDX_HEADS,
             H_B * DK_B, H_B * DK_B, H_B * DV_B, GATE_RANK, H_B * DV_B,
             H_C * HEAD_DIM, KV_C * HEAD_DIM, KV_C * HEAD_DIM)

LANES = 128
SUBLANES = 8
VMEM_LIMIT = 56 * 1024 * 1024

MXU_DTYPE = jnp.bfloat16
HALF_DTYPE = jnp.int16
F32 = jnp.float32
NEG_INF = float("-inf")
MASKED = -1e30
F32_MAX = float(jnp.finfo(jnp.float32).max)
INT_MIN = -2 ** 31
KEY_NEG_INF = INT_MIN + 2 ** 23
KEY_MIN_NORMAL = 2 ** 23
HALF_MIN = -2 ** 15

QBLK = 256
KV_EXT = KV_RANK + 16
ROW_CHUNK = 32
TM_PROJ = 512
TS_GLA = 512
GLA_SUB = 256
TQ_SWA = 512
TM_FFN = 512
TF_FFN = D_FF


def _dot(a, b):
    return jnp.dot(a, b, preferred_element_type=F32)


def _dot_nt(a, b):
    return lax.dot_general(a, b, (((1,), (1,)), ((), ())), preferred_element_type=F32)


def _dot_tn(a, b):
    return lax.dot_general(a, b, (((0,), (0,)), ((), ())), preferred_element_type=F32)


def _rms(x, g):
    return x * lax.rsqrt(jnp.mean(x * x, axis=-1, keepdims=True) + EPS) * g


_W_STARTS = (0, sum(IN_SPLITS[:5]), sum(IN_SPLITS[:9]))
_W_WIDTHS = (768, 1152, 1024)
_C_QA = (0, 0)
_C_CKV = (0, 256)
_C_QIDX = (0, 384)
_C_KW = (0, 640)
_C_QB = (1, 0)
_C_KB = (1, 256)
_C_VB = (1, 512)
_C_GLR = (1, 1024)
_C_RB = (2, 0)
_C_QC = (2, 512)
_C_KC = (2, 768)
_C_VC = (2, 896)


def _proj_kernel(x_ref, g_ref, wa_ref, wb_ref, wc_ref, wuk_ref, kvn_ref,
                 qlat_ref, ckv_ref, ckvt_ref, qidx_ref, kidx_ref, widxt_ref,
                 qb_ref, kb_ref, vb_ref, glr_ref, rb_ref, qc_ref, kc_ref, vc_ref):
    h = _rms(x_ref[...], g_ref[...]).astype(MXU_DTYPE)

    p = [_dot(h, w_ref[...]) for w_ref in (wa_ref, wb_ref, wc_ref)]

    def seg(where, width):
        window, c0 = where
        return p[window][:, c0:c0 + width]

    qa = seg(_C_QA, 256)
    for hd in range(H_A):
        qh = qa[:, hd * HEAD_DIM:(hd + 1) * HEAD_DIM].astype(MXU_DTYPE)
        ql = _dot(qh, wuk_ref[hd]) * (HEAD_DIM ** -0.5)
        for t in range(TM_PROJ // QBLK):
            qlat_ref[0, t, :, hd * QBLK:(hd + 1) * QBLK] = (
                ql[t * QBLK:(t + 1) * QBLK, :].T.astype(qlat_ref.dtype))

    ckv = _rms(seg(_C_CKV, KV_RANK), kvn_ref[...])
    ckv_ref[...] = ckv.astype(ckv_ref.dtype)
    for t in range(TM_PROJ // QBLK):
        ckvt_ref[0, t, 0:KV_RANK, :] = ckv[t * QBLK:(t + 1) * QBLK, :].T.astype(ckvt_ref.dtype)
        ckvt_ref[0, t, KV_RANK:, :] = jnp.ones((KV_EXT - KV_RANK, QBLK), ckvt_ref.dtype)

    qi = seg(_C_QIDX, 256)
    for t in range(TM_PROJ // QBLK):
        qit = qi[t * QBLK:(t + 1) * QBLK, :].T
        for hd in range(IDX_HEADS):
            qidx_ref[0, t, :, hd * QBLK:(hd + 1) * QBLK] = (
                qit[hd * IDX_DIM:(hd + 1) * IDX_DIM, :].astype(qidx_ref.dtype))
    kw = seg(_C_KW, LANES)
    kidx_ref[...] = kw[:, :IDX_DIM].astype(kidx_ref.dtype)
    kwt = kw.T
    widxt_ref[0] = kwt[IDX_DIM:IDX_DIM + SUBLANES, :] * (IDX_HEADS ** -0.5 * IDX_DIM ** -0.5)

    qb_ref[...] = seg(_C_QB, 256).astype(qb_ref.dtype)
    kb_ref[...] = seg(_C_KB, 256).astype(kb_ref.dtype)
    vb_ref[...] = seg(_C_VB, 512).astype(vb_ref.dtype)
    glr_ref[...] = seg(_C_GLR, LANES)
    rb_ref[...] = seg(_C_RB, 512).astype(rb_ref.dtype)
    qc_ref[...] = seg(_C_QC, 256).astype(qc_ref.dtype)
    kc_ref[...] = seg(_C_KC, LANES).astype(kc_ref.dtype)
    vc_ref[...] = seg(_C_VC, LANES).astype(vc_ref.dtype)


def _split_w_in(w):
    return tuple(w[:, c0:c0 + n] for c0, n in zip(_W_STARTS, _W_WIDTHS))


def _proj_call(x2, g, w_windows, wuk, kvn, bsz, s_len):
    t_tok = x2.shape[0]
    tm = TM_PROJ
    n_s = s_len // tm
    act = MXU_DTYPE

    def tok(width):
        return pl.BlockSpec((tm, width), lambda i: (i, 0))

    def const(shape):
        return pl.BlockSpec(shape, lambda i: (0,) * len(shape))

    def qblocks(depth):
        return pl.BlockSpec((1, tm // QBLK, depth, H_A * QBLK), lambda i: (i // n_s, i % n_s, 0, 0))

    out_shape = [
        jax.ShapeDtypeStruct((bsz, s_len // QBLK, KV_RANK, H_A * QBLK), act),
        jax.ShapeDtypeStruct((t_tok, KV_RANK), act),
        jax.ShapeDtypeStruct((bsz, s_len // QBLK, KV_EXT, QBLK), act),
        jax.ShapeDtypeStruct((bsz, s_len // QBLK, IDX_DIM, IDX_HEADS * QBLK), act),
        jax.ShapeDtypeStruct((t_tok, IDX_DIM), act),
        jax.ShapeDtypeStruct((bsz, SUBLANES, s_len), F32),
        jax.ShapeDtypeStruct((t_tok, H_B * DK_B), act),
        jax.ShapeDtypeStruct((t_tok, H_B * DK_B), act),
        jax.ShapeDtypeStruct((t_tok, H_B * DV_B), act),
        jax.ShapeDtypeStruct((t_tok, LANES), F32),
        jax.ShapeDtypeStruct((t_tok, H_B * DV_B), act),
        jax.ShapeDtypeStruct((t_tok, H_C * HEAD_DIM), act),
        jax.ShapeDtypeStruct((t_tok, KV_C * HEAD_DIM), act),
        jax.ShapeDtypeStruct((t_tok, KV_C * HEAD_DIM), act),
    ]
    out_specs = [
        qblocks(KV_RANK), tok(KV_RANK),
        pl.BlockSpec((1, tm // QBLK, KV_EXT, QBLK), lambda i: (i // n_s, i % n_s, 0, 0)),
        qblocks(IDX_DIM), tok(IDX_DIM),
        pl.BlockSpec((1, SUBLANES, tm), lambda i: (i // n_s, 0, i % n_s)),
        tok(H_B * DK_B), tok(H_B * DK_B), tok(H_B * DV_B), tok(LANES), tok(H_B * DV_B),
        tok(H_C * HEAD_DIM), tok(KV_C * HEAD_DIM), tok(KV_C * HEAD_DIM),
    ]
    return pl.pallas_call(
        _proj_kernel,
        grid=(t_tok // tm,),
        in_specs=[tok(D_MODEL), const((1, D_MODEL))]
                 + [const((D_MODEL, n)) for n in _W_WIDTHS] + [
                  const((H_A, HEAD_DIM, KV_RANK)), const((1, KV_RANK))],
        out_specs=out_specs,
        out_shape=out_shape,
        compiler_params=pltpu.CompilerParams(
            dimension_semantics=("parallel",), vmem_limit_bytes=VMEM_LIMIT),
        name="norm_in_proj",
    )(x2, g, *w_windows, wuk, kvn)


def _float_to_key(x):
    bits = lax.bitcast_convert_type(x, jnp.int32)
    return jnp.where(bits >= 0, bits, INT_MIN - bits)


def _key_to_float(k):
    k = jnp.where((k > 0) & (k < KEY_MIN_NORMAL), KEY_MIN_NORMAL, k)
    k = jnp.where((k < 0) & (k > -KEY_MIN_NORMAL), 0, k)
    bits = jnp.where(k >= 0, k, INT_MIN - k)
    return lax.bitcast_convert_type(bits, F32)


def _dsa_kernel(qidx_ref, widxt_ref, qlat_ref, kidx_ref, ckv_ref, ckvt_ref,
                biasn_ref, wuv_ref, out_ref,
                s_ref, hi_ref, lo_ref, lg_ref, acc_ref, da_ref, db_ref, pa_ref, pb_ref,
                da16_ref, *, s_len, topk):
    jq = pl.program_id(1)
    n_tiles = jq + 1
    n_virtual = s_len - n_tiles * QBLK
    hq = H_A * QBLK
    head_cols = [slice(hd * QBLK, (hd + 1) * QBLK) for hd in range(H_A)]

    def rows(j, r0=0, n=QBLK):
        return pl.ds(pl.multiple_of(j * QBLK + r0, n), n)

    def fold(x, op):
        return op(x.reshape(x.shape[0] // SUBLANES, SUBLANES, x.shape[-1]), axis=0)

    def pipeline(n, produce, consume, carry, bufs):
        a, b = bufs
        produce(jnp.maximum(n - 1, 0), a)

        def pair(i, carry):
            j = n - 1 - 2 * i
            produce(j - 1, b)
            carry = consume(j, a, carry)
            produce(jnp.maximum(j - 2, 0), a)
            return consume(j - 1, b, carry)

        carry = lax.fori_loop(0, n // 2, pair, carry)
        return lax.cond(n % 2 == 1, lambda c: consume(0, a, c), lambda c: c, carry)

    qis = qidx_ref[0, 0]
    wt = widxt_ref[0]

    def stage_scores(j, buf):
        buf[...] = _dot(kidx_ref[0, rows(j), :], qis)

    def put_scores(j, buf, carry, diagonal=False):
        for r0 in range(0, QBLK, ROW_CHUNK):
            rc = slice(r0, r0 + ROW_CHUNK)
            s = jnp.maximum(buf[rc, head_cols[0]], 0.0) * wt[0:1, :]
            for hd in range(1, IDX_HEADS):
                s = s + jnp.maximum(buf[rc, head_cols[hd]], 0.0) * wt[hd:hd + 1, :]
            if diagonal:
                kchunk = (r0 + lax.broadcasted_iota(jnp.int32, (ROW_CHUNK, 1), 0)) // CHUNK
                qchunk = lax.broadcasted_iota(jnp.int32, (1, QBLK), 1) // CHUNK
                s = jnp.where(kchunk <= qchunk, s, NEG_INF)
            s_ref[rows(j, r0, ROW_CHUNK), :] = s
            key = _float_to_key(s)
            hi_ref[rows(j, r0, ROW_CHUNK), :] = jnp.right_shift(key, 16).astype(HALF_DTYPE)
            lo_ref[rows(j, r0, ROW_CHUNK), :] = (jnp.bitwise_and(key, 0xFFFF) + HALF_MIN).astype(HALF_DTYPE)
        return carry

    stage_scores(jq, da_ref)
    put_scores(jq, da_ref, 0, diagonal=True)
    pipeline(jq, stage_scores, put_scores, 0, (da_ref, db_ref))

    def count_tiles(tile_count, zero):
        def quad(i, acc):
            j = 4 * i
            return acc + ((tile_count(j) + tile_count(j + 1)) + (tile_count(j + 2) + tile_count(j + 3)))

        acc = lax.fori_loop(0, n_tiles // 4, quad, zero)
        return lax.fori_loop(4 * (n_tiles // 4), n_tiles, lambda j, acc: acc + tile_count(j), acc)

    packed = 2 * SUBLANES

    def count_half(ref, t, strict=False):
        th = t.astype(HALF_DTYPE)
        one, zero = jnp.ones((), HALF_DTYPE), jnp.zeros((), HALF_DTYPE)

        def tile_count(j):
            x = ref[rows(j), :]
            hit = jnp.where(x > th if strict else x >= th, one, zero)
            slabs = [hit[r0:r0 + packed, :] for r0 in range(0, QBLK, packed)]
            while len(slabs) > 1:
                slabs = [a + b for a, b in zip(slabs[0::2], slabs[1::2])]
            return slabs[0]

        acc = count_tiles(tile_count, jnp.zeros((packed, QBLK), HALF_DTYPE))
        return jnp.sum(acc.astype(jnp.int32), axis=0, keepdims=True)

    def bisect_half(count_at_least, target):
        def step(i, t):
            cand = t + jnp.left_shift(jnp.int32(1), 15 - i)
            return jnp.where(count_at_least(cand) >= target, cand, t)

        return lax.fori_loop(0, 16, step, jnp.full((1, QBLK), HALF_MIN, jnp.int32))

    hi_neg_inf = KEY_NEG_INF >> 16
    g_hi = bisect_half(
        lambda t: count_half(hi_ref, t) + jnp.where(t <= hi_neg_inf, n_virtual, 0), topk)
    above = count_half(hi_ref, g_hi, strict=True) + jnp.where(g_hi < hi_neg_inf, n_virtual, 0)

    def mark_candidates(j, carry):
        g = g_hi.astype(HALF_DTYPE)
        da16_ref[rows(j), :] = jnp.where(hi_ref[rows(j), :] == g, lo_ref[rows(j), :],
                                         jnp.full((), HALF_MIN, HALF_DTYPE))
        return carry

    lax.fori_loop(0, n_tiles, mark_candidates, 0)
    g_lo = bisect_half(lambda t: count_half(da16_ref, t), topk - above)
    t_key = g_hi * 65536 + (g_lo - HALF_MIN)
    tf = _key_to_float(t_key)

    def count_ge(tf):
        acc = count_tiles(lambda j: fold(jnp.where(s_ref[rows(j), :] >= tf, 1, 0), jnp.sum),
                          jnp.zeros((SUBLANES, QBLK), jnp.int32))
        return jnp.sum(acc, axis=0, keepdims=True) + jnp.where(NEG_INF >= tf, n_virtual, 0)

    cnt_t = count_ge(tf)

    has_tie = jnp.max(jnp.where((cnt_t == topk) | (tf == NEG_INF), 0, 1))

    @pl.when(has_tie > 0)
    def _():
        t_next = _key_to_float(t_key + 1)
        need = (topk - count_ge(t_next)).astype(F32)
        r = lax.broadcasted_iota(jnp.int32, (QBLK, QBLK), 0)
        c = lax.broadcasted_iota(jnp.int32, (QBLK, QBLK), 1)
        tril = (c <= r).astype(MXU_DTYPE)

        def body(j, carry):
            tile = s_ref[rows(j), :]
            eq = (tile >= tf) & jnp.logical_not(tile >= t_next)
            pre = _dot(tril, eq.astype(MXU_DTYPE)) + carry
            s_ref[rows(j), :] = jnp.where(eq & (pre > need), NEG_INF, tile)
            return pre[QBLK - 1:QBLK, :]

        lax.fori_loop(0, n_tiles, body, jnp.zeros((1, QBLK), F32))

    t_sel = jnp.maximum(tf, -F32_MAX)
    def logits_tile(j, mx):
        ck = ckv_ref[0, rows(j), :]
        near = jnp.minimum(jq - j, 2)
        parts = []
        for hd in range(H_A):
            lg = _dot(ck, qlat_ref[0, 0, :, head_cols[hd]])
            pm = mx[:, head_cols[hd]]
            for r0 in range(0, QBLK, ROW_CHUNK):
                rc = slice(r0, r0 + ROW_CHUNK)
                sel = s_ref[rows(j, r0, ROW_CHUNK), :] >= t_sel
                v = jnp.where(sel, lg[rc, :] + biasn_ref[near, rc, head_cols[hd]], MASKED)
                lg_ref[rows(j, r0, ROW_CHUNK), head_cols[hd]] = v
                pm = jnp.maximum(pm, fold(v, jnp.max))
            parts.append(pm)
        return jnp.concatenate(parts, axis=-1)

    mx = lax.fori_loop(0, n_tiles // 2,
                       lambda i, mx: logits_tile(2 * i + 1, logits_tile(2 * i, mx)),
                       jnp.full((SUBLANES, hq), MASKED, F32))
    mx = lax.cond(n_tiles % 2 == 1, lambda mx: logits_tile(jq, mx), lambda mx: mx, mx)
    m = jnp.max(mx, axis=0, keepdims=True)

    acc_ref[...] = jnp.zeros(acc_ref.shape, F32)

    def stage_probs(j, buf):
        for r0 in range(0, QBLK, ROW_CHUNK):
            p = jnp.exp(lg_ref[rows(j, r0, ROW_CHUNK), :] - m)
            buf[r0:r0 + ROW_CHUNK, :] = p.astype(buf.dtype)

    def accumulate(j, buf, carry):
        acc_ref[...] += _dot(ckvt_ref[0, j], buf[...])
        return carry

    pipeline(n_tiles, stage_probs, accumulate, 0, (pa_ref, pb_ref))

    l = acc_ref[KV_RANK:KV_RANK + 1, :]
    outs = []
    for hd in range(H_A):
        o_t = acc_ref[0:KV_RANK, head_cols[hd]] / l[:, head_cols[hd]]
        outs.append(_dot(o_t.T.astype(MXU_DTYPE), wuv_ref[hd]))
    out_ref[0] = jnp.concatenate(outs, axis=-1).astype(out_ref.dtype)


def _dsa_call(qidx, widxt, qlat, kidx, ckv, ckvt, biasn, wuv, bsz, s_len):
    topk = min(TOPK_MAX, s_len // 4)
    nblk = s_len // QBLK
    qblock_t = lambda depth: pl.BlockSpec((1, 1, depth, H_A * QBLK), lambda b, i: (b, i, 0, 0))
    const = lambda shape: pl.BlockSpec(shape, lambda b, i: (0,) * len(shape))
    return pl.pallas_call(
        functools.partial(_dsa_kernel, s_len=s_len, topk=topk),
        grid=(bsz, nblk),
        in_specs=[
            qblock_t(IDX_DIM),
            pl.BlockSpec((1, SUBLANES, QBLK), lambda b, i: (b, 0, i)),
            qblock_t(KV_RANK),
            pl.BlockSpec((1, s_len, IDX_DIM), lambda b, i: (b, 0, 0)),
            pl.BlockSpec((1, s_len, KV_RANK), lambda b, i: (b, 0, 0)),
            pl.BlockSpec((1, nblk, KV_EXT, QBLK), lambda b, i: (b, 0, 0, 0)),
            const((3, QBLK, H_A * QBLK)),
            const((H_A, KV_RANK, HEAD_DIM)),
        ],
        out_specs=pl.BlockSpec((1, QBLK, H_A * HEAD_DIM), lambda b, i: (b, i, 0)),
        out_shape=jax.ShapeDtypeStruct((bsz, s_len, H_A * HEAD_DIM), MXU_DTYPE),
        scratch_shapes=[
            pltpu.VMEM((s_len, QBLK), F32),
            pltpu.VMEM((s_len, QBLK), HALF_DTYPE),
            pltpu.VMEM((s_len, QBLK), HALF_DTYPE),
            pltpu.VMEM((s_len, H_A * QBLK), F32),
            pltpu.VMEM((KV_EXT, H_A * QBLK), F32),
            pltpu.VMEM((QBLK, H_A * QBLK), F32),
            pltpu.VMEM((QBLK, H_A * QBLK), F32),
            pltpu.VMEM((QBLK, H_A * QBLK), MXU_DTYPE),
            pltpu.VMEM((QBLK, H_A * QBLK), MXU_DTYPE),
            pltpu.VMEM((s_len, QBLK), HALF_DTYPE),
        ],
        compiler_params=pltpu.CompilerParams(
            dimension_semantics=("parallel", "arbitrary"), vmem_limit_bytes=VMEM_LIMIT),
        name="dsa_mixer",
    )(qidx, widxt, qlat, kidx, ckv, ckvt, biasn, wuv)


def _split3(x):
    a = x.astype(MXU_DTYPE)
    r = x - a.astype(F32)
    b = r.astype(MXU_DTYPE)
    c = (r - b.astype(F32)).astype(MXU_DTYPE)
    return a, b, c


def _gla_kernel(qb_ref, kb_ref, vb_ref, glr_ref, rb_ref, wg_ref, bg_ref, gn_ref,
                out_ref, st_ref):
    n_chunks = GLA_SUB // CHUNK
    assert n_chunks * DK_B == GLA_SUB

    @pl.when(pl.program_id(1) == 0)
    def _():
        st_ref[...] = jnp.zeros(st_ref.shape, F32)

    r = lax.broadcasted_iota(jnp.int32, (GLA_SUB, GLA_SUB), 0)
    c = lax.broadcasted_iota(jnp.int32, (GLA_SUB, GLA_SUB), 1)
    same_chunk = (r // CHUNK) == (c // CHUNK)
    causal = same_chunk & (c <= r)
    in_block = (r // CHUNK) == (c // DK_B)
    tril = causal.astype(MXU_DTYPE)
    ones = same_chunk.astype(MXU_DTYPE)
    w_hi, w_lo, _ = _split3(wg_ref[...])

    def expand(x):
        return jnp.where(in_block, jnp.concatenate([x] * n_chunks, axis=1), 0.0).astype(MXU_DTYPE)

    def decays(rows):
        g_hi, g_lo, _ = _split3(glr_ref[0, rows, :])
        z = _dot(g_hi, w_hi) + (_dot(g_hi, w_lo) + _dot(g_lo, w_hi)) + bg_ref[...]
        g = (jnp.minimum(z, 0.0) - jnp.log1p(jnp.exp(-jnp.abs(z)))) / GATE_TEMP
        g1, g2, g3 = _split3(g)
        b = _dot(tril, g1) + (_dot(tril, g2) + _dot(tril, g3))
        b_last = _dot(ones, g1) + (_dot(ones, g2) + _dot(ones, g3))
        q = qb_ref[0, rows, :].astype(F32) * (DK_B ** -0.5)
        k = kb_ref[0, rows, :].astype(F32)
        qe = q * jnp.exp(b)
        ke = (k * jnp.exp(-b)).astype(MXU_DTYPE)
        kd = k * jnp.exp(b_last - b)
        return qe, ke, kd, jnp.exp(b_last)

    prepared = [decays(slice(t * GLA_SUB, (t + 1) * GLA_SUB)) for t in range(TS_GLA // GLA_SUB)]
    state = [st_ref[hd] for hd in range(H_B)]
    for t, (qe, ke, kd, decay) in enumerate(prepared):
        rows = slice(t * GLA_SUB, (t + 1) * GLA_SUB)
        for hd in range(H_B):
            ks = slice(hd * DK_B, (hd + 1) * DK_B)
            vs = slice(hd * DV_B, (hd + 1) * DV_B)
            v = vb_ref[0, rows, vs]
            qe_h = qe[:, ks]
            att = jnp.where(causal, _dot_nt(qe_h.astype(MXU_DTYPE), ke[:, ks]), 0.0)
            kv_t = _dot_tn(v, expand(kd[:, ks]))
            st = state[hd]
            starts = []
            for ci in range(n_chunks):
                starts.append(st)
                st = (st * decay[ci * CHUNK:ci * CHUNK + 1, ks]
                      + kv_t[:, ci * DK_B:(ci + 1) * DK_B])
            state[hd] = st
            st_all = jnp.concatenate(starts, axis=1).astype(MXU_DTYPE)
            o = _dot(att.astype(MXU_DTYPE), v) + _dot_nt(expand(qe_h), st_all)
            o = o * lax.rsqrt(jnp.mean(o * o, axis=-1, keepdims=True) + EPS) * gn_ref[:, vs]
            rg = rb_ref[0, rows, vs].astype(F32)
            o = o * (rg / (1.0 + jnp.exp(-rg)))
            out_ref[0, rows, vs] = o.astype(out_ref.dtype)
    for hd in range(H_B):
        st_ref[hd] = state[hd]


def _gla_call(qb, kb, vb, glr, rb, wg, bg, gn, bsz, s_len):
    ts = TS_GLA
    tok = lambda width: pl.BlockSpec((1, ts, width), lambda b, i: (b, i, 0))
    const = lambda shape: pl.BlockSpec(shape, lambda b, i: (0,) * len(shape))
    return pl.pallas_call(
        _gla_kernel,
        grid=(bsz, s_len // ts),
        in_specs=[tok(H_B * DK_B), tok(H_B * DK_B), tok(H_B * DV_B), tok(LANES), tok(H_B * DV_B),
                  const((LANES, H_B * DK_B)), const((1, H_B * DK_B)), const((1, H_B * DV_B))],
        out_specs=tok(H_B * DV_B),
        out_shape=jax.ShapeDtypeStruct((bsz, s_len, H_B * DV_B), MXU_DTYPE),
        scratch_shapes=[pltpu.VMEM((H_B, DV_B, DK_B), F32)],
        compiler_params=pltpu.CompilerParams(
            dimension_semantics=("parallel", "arbitrary"), vmem_limit_bytes=VMEM_LIMIT),
        name="gla_mixer",
    )(qb, kb, vb, glr, rb, wg, bg, gn)


SWA_SUB = WINDOW
SWA_WIN = SWA_SUB + WINDOW


def _swa_window_start(tile_start, sub):
    return jnp.maximum(tile_start + sub * SWA_SUB - WINDOW, 0)


def _swa_kernel(sink_ref, qc_ref, kc_ref, vc_ref, mb_ref, out_ref):
    i = pl.program_id(1)
    grp = H_C // KV_C
    for sub in range(TQ_SWA // SWA_SUB):
        qrows = slice(sub * SWA_SUB, (sub + 1) * SWA_SUB)
        start = pl.multiple_of(_swa_window_start(i * TQ_SWA, sub), WINDOW)
        k = kc_ref[0, pl.ds(start, SWA_WIN), :]
        v = vc_ref[0, pl.ds(start, SWA_WIN), :]
        outs = []
        for hd in range(H_C):
            kv = hd // grp
            q = qc_ref[0, qrows, hd * HEAD_DIM:(hd + 1) * HEAD_DIM]
            lg = (_dot_nt(q, k[:, kv * HEAD_DIM:(kv + 1) * HEAD_DIM]) * (HEAD_DIM ** -0.5)
                  + mb_ref[0, hd, sub])
            sink = sink_ref[hd]
            m = jnp.maximum(jnp.max(lg, axis=-1, keepdims=True), sink)
            e = jnp.exp(lg - m)
            p = e / (jnp.sum(e, axis=-1, keepdims=True) + jnp.exp(sink - m))
            outs.append(_dot(p.astype(MXU_DTYPE), v[:, kv * HEAD_DIM:(kv + 1) * HEAD_DIM]))
        out_ref[0, qrows, :] = jnp.concatenate(outs, axis=-1).astype(out_ref.dtype)


def _swa_call(sinks, qc, kc, vc, maskbias, bsz, s_len):
    tq = TQ_SWA
    return pl.pallas_call(
        _swa_kernel,
        grid=(bsz, s_len // tq),
        in_specs=[
            pl.BlockSpec(memory_space=pltpu.SMEM),
            pl.BlockSpec((1, tq, H_C * HEAD_DIM), lambda b, i: (b, i, 0)),
            pl.BlockSpec((1, s_len, KV_C * HEAD_DIM), lambda b, i: (b, 0, 0)),
            pl.BlockSpec((1, s_len, KV_C * HEAD_DIM), lambda b, i: (b, 0, 0)),
            pl.BlockSpec((1, H_C, tq // SWA_SUB, SWA_SUB, SWA_WIN),
                         lambda b, i: (jnp.minimum(i, 1), 0, 0, 0, 0)),
        ],
        out_specs=pl.BlockSpec((1, tq, H_C * HEAD_DIM), lambda b, i: (b, i, 0)),
        out_shape=jax.ShapeDtypeStruct((bsz, s_len, H_C * HEAD_DIM), MXU_DTYPE),
        compiler_params=pltpu.CompilerParams(
            dimension_semantics=("parallel", "arbitrary"), vmem_limit_bytes=VMEM_LIMIT),
        name="swa_mixer",
    )(sinks, qc, kc, vc, maskbias)


def _ffn_kernel(x_ref, oa_ref, ob_ref, oc_ref, gf_ref, gfin_ref,
                wo_hbm, wg_hbm, wu_hbm, wd_hbm, out_ref,
                wo_ref, wg_ref, wu_ref, wd_ref, sems, *, final_norm):
    @pl.when(pl.program_id(0) == 0)
    def _():
        pairs = ((wo_hbm, wo_ref), (wg_hbm, wg_ref), (wu_hbm, wu_ref), (wd_hbm, wd_ref))
        copies = [pltpu.make_async_copy(src, dst, sems.at[k]) for k, (src, dst) in enumerate(pairs)]
        for cp in copies:
            cp.start()
        for cp in copies:
            cp.wait()

    na = H_A * HEAD_DIM
    nb = H_B * DV_B
    x1 = (x_ref[...] + _dot(oa_ref[...], wo_ref[0:na, :])
          + _dot(ob_ref[...], wo_ref[na:na + nb, :])
          + _dot(oc_ref[...], wo_ref[na + nb:, :]))
    h = _rms(x1, gf_ref[...]).astype(MXU_DTYPE)
    y = x1
    for j in range(D_FF // TF_FFN):
        cols = slice(j * TF_FFN, (j + 1) * TF_FFN)
        gate = _dot(h, wg_ref[:, cols])
        up = _dot(h, wu_ref[:, cols])
        a = (gate / (1.0 + jnp.exp(-gate))) * up
        y = y + _dot(a.astype(MXU_DTYPE), wd_ref[cols, :])
    if final_norm:
        y = _rms(y, gfin_ref[...])
    out_ref[...] = y


def _ffn_call(x2, oa, ob, oc, wo, gf, wg, wu, wd, gfin, final_norm):
    t_tok = x2.shape[0]
    tm = TM_FFN
    tok = lambda width: pl.BlockSpec((tm, width), lambda i: (i, 0))
    const = lambda shape: pl.BlockSpec(shape, lambda i: (0,) * len(shape))
    in_hbm = pl.BlockSpec(memory_space=pl.ANY)
    return pl.pallas_call(
        functools.partial(_ffn_kernel, final_norm=final_norm),
        grid=(t_tok // tm,),
        in_specs=[tok(D_MODEL), tok(H_A * HEAD_DIM), tok(H_B * DV_B), tok(H_C * HEAD_DIM),
                  const((1, D_MODEL)), const((1, D_MODEL)), in_hbm, in_hbm, in_hbm, in_hbm],
        out_specs=tok(D_MODEL),
        out_shape=jax.ShapeDtypeStruct((t_tok, D_MODEL), F32),
        scratch_shapes=[pltpu.VMEM((D_MODEL, D_MODEL), MXU_DTYPE),
                        pltpu.VMEM((D_MODEL, D_FF), MXU_DTYPE),
                        pltpu.VMEM((D_MODEL, D_FF), MXU_DTYPE),
                        pltpu.VMEM((D_FF, D_MODEL), MXU_DTYPE),
                        pltpu.SemaphoreType.DMA((4,))],
        compiler_params=pltpu.CompilerParams(
            dimension_semantics=("arbitrary",), vmem_limit_bytes=VMEM_LIMIT),
        name="out_proj_ffn",
    )(x2, oa, ob, oc, gf, gfin, wo, wg, wu, wd)


def _rel_bucket(rel):
    nb = NUM_BUCKETS // 2
    max_exact = nb // 2
    n = jnp.abs(rel)
    side = jnp.where(rel > 0, nb, 0)
    nf = jnp.maximum(n, 1).astype(jnp.float32)
    large = max_exact + (jnp.log(nf / max_exact) / math.log(MAX_DISTANCE / max_exact)
                         * (nb - max_exact)).astype(jnp.int32)
    large = jnp.minimum(large, nb - 1)
    return side + jnp.where(n < max_exact, n, large)


def _bucket_ids(ndim, axis):
    shape = [1] * ndim
    shape[axis] = NUM_BUCKETS
    return jnp.arange(NUM_BUCKETS, dtype=jnp.int32).reshape(shape)


def _dsa_bias_tables(rel_bias):
    table = rel_bias[:, :H_A].astype(F32)
    far = _rel_bucket(jnp.full((), -2 * QBLK, jnp.int32))
    table = table - jnp.sum(jnp.where(_bucket_ids(2, 0) == far, table, 0.0), axis=0, keepdims=True)
    kk = jnp.arange(QBLK, dtype=jnp.int32)[:, None]
    qq = jnp.arange(QBLK, dtype=jnp.int32)[None, :]
    near = []
    for shift in (0, QBLK):
        bucket = _rel_bucket(kk - shift - qq)[:, None, None, :]
        b = jnp.sum(jnp.where(bucket == _bucket_ids(4, 1), table[None, :, :, None], 0.0), axis=1)
        near.append(b.reshape(QBLK, H_A * QBLK))
    near.append(jnp.zeros((QBLK, H_A * QBLK), F32))
    return jnp.stack(near)


def _swa_maskbias(rel_bias):
    table = rel_bias[:, H_A:].astype(F32)
    out = []
    for tile_start in (0, TQ_SWA):
        subs = []
        for sub in range(TQ_SWA // SWA_SUB):
            qq = tile_start + sub * SWA_SUB + jnp.arange(SWA_SUB, dtype=jnp.int32)[:, None]
            kk = (max(tile_start + sub * SWA_SUB - WINDOW, 0)
                  + jnp.arange(SWA_WIN, dtype=jnp.int32)[None, :])
            dc = qq // CHUNK - kk // CHUNK
            band = (dc >= 0) & (dc <= WIN_CHUNKS)
            bucket = _rel_bucket(kk - qq)[None, None]
            b = jnp.sum(jnp.where(bucket == _bucket_ids(4, 0), table[:, :, None, None], 0.0), axis=0)
            subs.append(jnp.where(band[None], b, NEG_INF))
        out.append(jnp.stack(subs, axis=1))
    return jnp.stack(out)


def kernel(x, w_in, w_out, norm_mix, norm_ffn, kv_norm, w_uk, w_uv, w_gate2, b_gate,
           gla_norm, sinks, rel_bias, w_ffn_gate, w_ffn_up, w_ffn_down, final_norm):
    bsz, s_len, _ = x.shape
    t_tok = bsz * s_len
    wd = MXU_DTYPE
    biasn = _dsa_bias_tables(rel_bias)
    maskbias = _swa_maskbias(rel_bias)
    x2 = x.reshape(t_tok, D_MODEL)
    row = lambda v: v.reshape(1, -1).astype(F32)
    for l in range(DEPTH):
        w_windows = _split_w_in(w_in[l].astype(wd))
        wuk = jnp.transpose(w_uk[l], (1, 2, 0)).astype(wd)
        wuv = jnp.transpose(w_uv[l], (1, 0, 2)).astype(wd)
        (qlat, ckv, ckvt, qidx, kidx, widxt, qb, kb, vb, glr, rb, qc, kc, vc) = _proj_call(
            x2, row(norm_mix[l]), w_windows, wuk, row(kv_norm[l]), bsz, s_len)
        b3 = lambda a: a.reshape(bsz, s_len, a.shape[-1])
        out_a = _dsa_call(qidx, widxt, qlat, b3(kidx), b3(ckv), ckvt,
                          biasn, wuv, bsz, s_len)
        wg2 = jnp.zeros((LANES, H_B * DK_B), F32).at[:GATE_RANK].set(w_gate2[l])
        out_b = _gla_call(b3(qb), b3(kb), b3(vb), b3(glr), b3(rb), wg2,
                          row(b_gate[l]), row(gla_norm[l]), bsz, s_len)
        out_c = _swa_call(sinks[l].astype(F32), b3(qc), b3(kc), b3(vc), maskbias, bsz, s_len)
        f2 = lambda a: a.reshape(t_tok, a.shape[-1])
        x2 = _ffn_call(x2, f2(out_a), f2(out_b), f2(out_c), w_out[l].astype(wd),
                       row(norm_ffn[l]), w_ffn_gate[l].astype(wd), w_ffn_up[l].astype(wd),
                       w_ffn_down[l].astype(wd), row(final_norm), l == DEPTH - 1)
    return x2.reshape(bsz, s_len, D_MODEL)
```

```python
import functools
import math

import jax
import jax.numpy as jnp
from jax import lax
from jax.experimental import pallas as pl
from jax.experimental.pallas import tpu as pltpu

D_MODEL = 1024
DEPTH = 2
CHUNK = 64
HEAD_DIM = 64
H_A = 4
KV_RANK = 128
IDX_HEADS = 4
IDX_DIM = 64
TOPK_MAX = 256
H_B = 4
DK_B = 64
DV_B = 128
GATE_RANK = 16
GATE_TEMP = 16.0
H_C = 4
KV_C = 2
WINDOW = 128
WIN_CHUNKS = WINDOW // CHUNK
NUM_BUCKETS = 32
MAX_DISTANCE = 128
D_FF = ((8 * D_MODEL // 3 + 255) // 256) * 256
EPS = 1e-6

IN_SPLITS = (H_A * HEAD_DIM, KV_RANK, IDX_HEADS * IDX_DIM, IDX_DIM, IDX_HEADS,
             H_B * DK_B, H_B * DK_B, H_B * DV_B, GATE_RANK, H_B * DV_B,
             H_C * HEAD_DIM, KV_C * HEAD_DIM, KV_C * HEAD_DIM)

LANES = 128
SUBLANES = 8
VMEM_LIMIT = 56 * 1024 * 1024

MXU_DTYPE = jnp.bfloat16
HALF_DTYPE = jnp.int16
F32 = jnp.float32
NEG_INF = float("-inf")
MASKED = -1e30
F32_MAX = float(jnp.finfo(jnp.float32).max)
INT_MIN = -2 ** 31
KEY_NEG_INF = INT_MIN + 2 ** 23
KEY_MIN_NORMAL = 2 ** 23
HALF_MIN = -2 ** 15

QBLK = 256
KV_EXT = KV_RANK + 16
ROW_CHUNK = 32
TM_PROJ = 512
TS_GLA = 512
GLA_SUB = 128
TQ_SWA = 512
TM_FFN = 512
TF_FFN = D_FF


def _dot(a, b):
    return jnp.dot(a, b, preferred_element_type=F32)


def _dot_nt(a, b):
    return lax.dot_general(a, b, (((1,), (1,)), ((), ())), preferred_element_type=F32)


def _dot_tn(a, b):
    return lax.dot_general(a, b, (((0,), (0,)), ((), ())), preferred_element_type=F32)


def _rms(x, g):
    return x * lax.rsqrt(jnp.mean(x * x, axis=-1, keepdims=True) + EPS) * g


_W_STARTS = (0, sum(IN_SPLITS[:5]), sum(IN_SPLITS[:9]))
_W_WIDTHS = (768, 1152, 1024)
_C_QA = (0, 0)
_C_CKV = (0, 256)
_C_QIDX = (0, 384)
_C_KW = (0, 640)
_C_QB = (1, 0)
_C_KB = (1, 256)
_C_VB = (1, 512)
_C_GLR = (1, 1024)
_C_RB = (2, 0)
_C_QC = (2, 512)
_C_KC = (2, 768)
_C_VC = (2, 896)


def _proj_kernel(x_ref, g_ref, wa_ref, wb_ref, wc_ref, wuk_ref, kvn_ref,
                 qlat_ref, ckv_ref, ckvt_ref, qidx_ref, kidx_ref, widxt_ref,
                 qb_ref, kb_ref, vb_ref, glr_ref, rb_ref, qc_ref, kc_ref, vc_ref):
    h = _rms(x_ref[...], g_ref[...]).astype(MXU_DTYPE)

    p = [_dot(h, w_ref[...]) for w_ref in (wa_ref, wb_ref, wc_ref)]

    def seg(where, width):
        window, c0 = where
        return p[window][:, c0:c0 + width]

    qa = seg(_C_QA, 256)
    for hd in range(H_A):
        qh = qa[:, hd * HEAD_DIM:(hd + 1) * HEAD_DIM].astype(MXU_DTYPE)
        ql = _dot(qh, wuk_ref[hd]) * (HEAD_DIM ** -0.5)
        for t in range(TM_PROJ // QBLK):
            qlat_ref[0, t, :, hd * QBLK:(hd + 1) * QBLK] = (
                ql[t * QBLK:(t + 1) * QBLK, :].T.astype(qlat_ref.dtype))

    ckv = _rms(seg(_C_CKV, KV_RANK), kvn_ref[...])
    ckv_ref[...] = ckv.astype(ckv_ref.dtype)
    for t in range(TM_PROJ // QBLK):
        ckvt_ref[0, t, 0:KV_RANK, :] = ckv[t * QBLK:(t + 1) * QBLK, :].T.astype(ckvt_ref.dtype)
        ckvt_ref[0, t, KV_RANK:, :] = jnp.ones((KV_EXT - KV_RANK, QBLK), ckvt_ref.dtype)

    qi = seg(_C_QIDX, 256)
    for t in range(TM_PROJ // QBLK):
        qit = qi[t * QBLK:(t + 1) * QBLK, :].T
        for hd in range(IDX_HEADS):
            qidx_ref[0, t, :, hd * QBLK:(hd + 1) * QBLK] = (
                qit[hd * IDX_DIM:(hd + 1) * IDX_DIM, :].astype(qidx_ref.dtype))
    kw = seg(_C_KW, LANES)
    kidx_ref[...] = kw[:, :IDX_DIM].astype(kidx_ref.dtype)
    kwt = kw.T
    widxt_ref[0] = kwt[IDX_DIM:IDX_DIM + SUBLANES, :] * (IDX_HEADS ** -0.5 * IDX_DIM ** -0.5)

    qb_ref[...] = seg(_C_QB, 256).astype(qb_ref.dtype)
    kb_ref[...] = seg(_C_KB, 256).astype(kb_ref.dtype)
    vb_ref[...] = seg(_C_VB, 512).astype(vb_ref.dtype)
    glr_ref[...] = seg(_C_GLR, LANES)
    rb_ref[...] = seg(_C_RB, 512).astype(rb_ref.dtype)
    qc_ref[...] = seg(_C_QC, 256).astype(qc_ref.dtype)
    kc_ref[...] = seg(_C_KC, LANES).astype(kc_ref.dtype)
    vc_ref[...] = seg(_C_VC, LANES).astype(vc_ref.dtype)


def _split_w_in(w):
    return tuple(w[:, c0:c0 + n] for c0, n in zip(_W_STARTS, _W_WIDTHS))


def _proj_call(x2, g, w_windows, wuk, kvn, bsz, s_len):
    t_tok = x2.shape[0]
    tm = TM_PROJ
    n_s = s_len // tm
    act = MXU_DTYPE

    def tok(width):
        return pl.BlockSpec((tm, width), lambda i: (i, 0))

    def const(shape):
        return pl.BlockSpec(shape, lambda i: (0,) * len(shape))

    def qblocks(depth):
        return pl.BlockSpec((1, tm // QBLK, depth, H_A * QBLK), lambda i: (i // n_s, i % n_s, 0, 0))

    out_shape = [
        jax.ShapeDtypeStruct((bsz, s_len // QBLK, KV_RANK, H_A * QBLK), act),
        jax.ShapeDtypeStruct((t_tok, KV_RANK), act),
        jax.ShapeDtypeStruct((bsz, s_len // QBLK, KV_EXT, QBLK), act),
        jax.ShapeDtypeStruct((bsz, s_len // QBLK, IDX_DIM, IDX_HEADS * QBLK), act),
        jax.ShapeDtypeStruct((t_tok, IDX_DIM), act),
        jax.ShapeDtypeStruct((bsz, SUBLANES, s_len), F32),
        jax.ShapeDtypeStruct((t_tok, H_B * DK_B), act),
        jax.ShapeDtypeStruct((t_tok, H_B * DK_B), act),
        jax.ShapeDtypeStruct((t_tok, H_B * DV_B), act),
        jax.ShapeDtypeStruct((t_tok, LANES), F32),
        jax.ShapeDtypeStruct((t_tok, H_B * DV_B), act),
        jax.ShapeDtypeStruct((t_tok, H_C * HEAD_DIM), act),
        jax.ShapeDtypeStruct((t_tok, KV_C * HEAD_DIM), act),
        jax.ShapeDtypeStruct((t_tok, KV_C * HEAD_DIM), act),
    ]
    out_specs = [
        qblocks(KV_RANK), tok(KV_RANK),
        pl.BlockSpec((1, tm // QBLK, KV_EXT, QBLK), lambda i: (i // n_s, i % n_s, 0, 0)),
        qblocks(IDX_DIM), tok(IDX_DIM),
        pl.BlockSpec((1, SUBLANES, tm), lambda i: (i // n_s, 0, i % n_s)),
        tok(H_B * DK_B), tok(H_B * DK_B), tok(H_B * DV_B), tok(LANES), tok(H_B * DV_B),
        tok(H_C * HEAD_DIM), tok(KV_C * HEAD_DIM), tok(KV_C * HEAD_DIM),
    ]
    return pl.pallas_call(
        _proj_kernel,
        grid=(t_tok // tm,),
        in_specs=[tok(D_MODEL), const((1, D_MODEL))]
                 + [const((D_MODEL, n)) for n in _W_WIDTHS] + [
                  const((H_A, HEAD_DIM, KV_RANK)), const((1, KV_RANK))],
        out_specs=out_specs,
        out_shape=out_shape,
        compiler_params=pltpu.CompilerParams(
            dimension_semantics=("parallel",), vmem_limit_bytes=VMEM_LIMIT),
        name="norm_in_proj",
    )(x2, g, *w_windows, wuk, kvn)


def _float_to_key(x):
    bits = lax.bitcast_convert_type(x, jnp.int32)
    return jnp.where(bits >= 0, bits, INT_MIN - bits)


def _key_to_float(k):
    k = jnp.where((k > 0) & (k < KEY_MIN_NORMAL), KEY_MIN_NORMAL, k)
    k = jnp.where((k < 0) & (k > -KEY_MIN_NORMAL), 0, k)
    bits = jnp.where(k >= 0, k, INT_MIN - k)
    return lax.bitcast_convert_type(bits, F32)


def _dsa_kernel(qidx_ref, widxt_ref, qlat_ref, kidx_ref, ckv_ref, ckvt_ref,
                biasn_ref, wuv_ref, out_ref,
                s_ref, hi_ref, lo_ref, lg_ref, acc_ref, da_ref, db_ref, pa_ref, pb_ref,
                da16_ref, *, s_len, topk):
    jq = pl.program_id(1)
    n_tiles = jq + 1
    n_virtual = s_len - n_tiles * QBLK
    hq = H_A * QBLK
    head_cols = [slice(hd * QBLK, (hd + 1) * QBLK) for hd in range(H_A)]

    def rows(j, r0=0, n=QBLK):
        return pl.ds(pl.multiple_of(j * QBLK + r0, n), n)

    def fold(x, op):
        return op(x.reshape(x.shape[0] // SUBLANES, SUBLANES, x.shape[-1]), axis=0)

    def pipeline(n, produce, consume, carry, bufs):
        a, b = bufs
        produce(jnp.maximum(n - 1, 0), a)

        def pair(i, carry):
            j = n - 1 - 2 * i
            produce(j - 1, b)
            carry = consume(j, a, carry)
            produce(jnp.maximum(j - 2, 0), a)
            return consume(j - 1, b, carry)

        carry = lax.fori_loop(0, n // 2, pair, carry)
        return lax.cond(n % 2 == 1, lambda c: consume(0, a, c), lambda c: c, carry)

    qis = qidx_ref[0, 0]
    wt = widxt_ref[0]

    def stage_scores(j, buf):
        buf[...] = _dot(kidx_ref[0, rows(j), :], qis)

    def put_scores(j, buf, carry, diagonal=False):
        for r0 in range(0, QBLK, ROW_CHUNK):
            rc = slice(r0, r0 + ROW_CHUNK)
            s = jnp.maximum(buf[rc, head_cols[0]], 0.0) * wt[0:1, :]
            for hd in range(1, IDX_HEADS):
                s = s + jnp.maximum(buf[rc, head_cols[hd]], 0.0) * wt[hd:hd + 1, :]
            if diagonal:
                kchunk = (r0 + lax.broadcasted_iota(jnp.int32, (ROW_CHUNK, 1), 0)) // CHUNK
                qchunk = lax.broadcasted_iota(jnp.int32, (1, QBLK), 1) // CHUNK
                s = jnp.where(kchunk <= qchunk, s, NEG_INF)
            s_ref[rows(j, r0, ROW_CHUNK), :] = s
            key = _float_to_key(s)
            hi_ref[rows(j, r0, ROW_CHUNK), :] = jnp.right_shift(key, 16).astype(HALF_DTYPE)
            lo_ref[rows(j, r0, ROW_CHUNK), :] = (jnp.bitwise_and(key, 0xFFFF) + HALF_MIN).astype(HALF_DTYPE)
        return carry

    stage_scores(jq, da_ref)
    put_scores(jq, da_ref, 0, diagonal=True)
    pipeline(jq, stage_scores, put_scores, 0, (da_ref, db_ref))

    def count_tiles(tile_count, zero):
        def quad(i, acc):
            j = 4 * i
            return acc + ((tile_count(j) + tile_count(j + 1)) + (tile_count(j + 2) + tile_count(j + 3)))

        acc = lax.fori_loop(0, n_tiles // 4, quad, zero)
        return lax.fori_loop(4 * (n_tiles // 4), n_tiles, lambda j, acc: acc + tile_count(j), acc)

    packed = 2 * SUBLANES

    def count_half(ref, t, strict=False):
        th = t.astype(HALF_DTYPE)
        one, zero = jnp.ones((), HALF_DTYPE), jnp.zeros((), HALF_DTYPE)

        def tile_count(j):
            x = ref[rows(j), :]
            hit = jnp.where(x > th if strict else x >= th, one, zero)
            slabs = [hit[r0:r0 + packed, :] for r0 in range(0, QBLK, packed)]
            while len(slabs) > 1:
                slabs = [a + b for a, b in zip(slabs[0::2], slabs[1::2])]
            return slabs[0]

        acc = count_tiles(tile_count, jnp.zeros((packed, QBLK), HALF_DTYPE))
        return jnp.sum(acc.astype(jnp.int32), axis=0, keepdims=True)

    def bisect_half(count_at_least, target):
        def step(i, t):
            cand = t + jnp.left_shift(jnp.int32(1), 15 - i)
            return jnp.where(count_at_least(cand) >= target, cand, t)

        return lax.fori_loop(0, 16, step, jnp.full((1, QBLK), HALF_MIN, jnp.int32))

    hi_neg_inf = KEY_NEG_INF >> 16
    g_hi = bisect_half(
        lambda t: count_half(hi_ref, t) + jnp.where(t <= hi_neg_inf, n_virtual, 0), topk)
    above = count_half(hi_ref, g_hi, strict=True) + jnp.where(g_hi < hi_neg_inf, n_virtual, 0)

    def mark_candidates(j, carry):
        g = g_hi.astype(HALF_DTYPE)
        da16_ref[rows(j), :] = jnp.where(hi_ref[rows(j), :] == g, lo_ref[rows(j), :],
                                         jnp.full((), HALF_MIN, HALF_DTYPE))
        return carry

    lax.fori_loop(0, n_tiles, mark_candidates, 0)
    g_lo = bisect_half(lambda t: count_half(da16_ref, t), topk - above)
    t_key = g_hi * 65536 + (g_lo - HALF_MIN)
    tf = _key_to_float(t_key)

    def count_ge(tf):
        acc = count_tiles(lambda j: fold(jnp.where(s_ref[rows(j), :] >= tf, 1, 0), jnp.sum),
                          jnp.zeros((SUBLANES, QBLK), jnp.int32))
        return jnp.sum(acc, axis=0, keepdims=True) + jnp.where(NEG_INF >= tf, n_virtual, 0)

    cnt_t = count_ge(tf)

    has_tie = jnp.max(jnp.where((cnt_t == topk) | (tf == NEG_INF), 0, 1))

    @pl.when(has_tie > 0)
    def _():
        t_next = _key_to_float(t_key + 1)
        need = (topk - count_ge(t_next)).astype(F32)
        r = lax.broadcasted_iota(jnp.int32, (QBLK, QBLK), 0)
        c = lax.broadcasted_iota(jnp.int32, (QBLK, QBLK), 1)
        tril = (c <= r).astype(MXU_DTYPE)

        def body(j, carry):
            tile = s_ref[rows(j), :]
            eq = (tile >= tf) & jnp.logical_not(tile >= t_next)
            pre = _dot(tril, eq.astype(MXU_DTYPE)) + carry
            s_ref[rows(j), :] = jnp.where(eq & (pre > need), NEG_INF, tile)
            return pre[QBLK - 1:QBLK, :]

        lax.fori_loop(0, n_tiles, body, jnp.zeros((1, QBLK), F32))

    t_sel = jnp.maximum(tf, -F32_MAX)
    def logits_tile(j, mx):
        ck = ckv_ref[0, rows(j), :]
        near = jnp.minimum(jq - j, 2)
        parts = []
        for hd in range(H_A):
            lg = _dot(ck, qlat_ref[0, 0, :, head_cols[hd]])
            pm = mx[:, head_cols[hd]]
            for r0 in range(0, QBLK, ROW_CHUNK):
                rc = slice(r0, r0 + ROW_CHUNK)
                sel = s_ref[rows(j, r0, ROW_CHUNK), :] >= t_sel
                v = jnp.where(sel, lg[rc, :] + biasn_ref[near, rc, head_cols[hd]], MASKED)
                lg_ref[rows(j, r0, ROW_CHUNK), head_cols[hd]] = v
                pm = jnp.maximum(pm, fold(v, jnp.max))
            parts.append(pm)
        return jnp.concatenate(parts, axis=-1)

    mx = lax.fori_loop(0, n_tiles // 2,
                       lambda i, mx: logits_tile(2 * i + 1, logits_tile(2 * i, mx)),
                       jnp.full((SUBLANES, hq), MASKED, F32))
    mx = lax.cond(n_tiles % 2 == 1, lambda mx: logits_tile(jq, mx), lambda mx: mx, mx)
    m = jnp.max(mx, axis=0, keepdims=True)

    acc_ref[...] = jnp.zeros(acc_ref.shape, F32)

    def stage_probs(j, buf):
        for r0 in range(0, QBLK, ROW_CHUNK):
            p = jnp.exp(lg_ref[rows(j, r0, ROW_CHUNK), :] - m)
            buf[r0:r0 + ROW_CHUNK, :] = p.astype(buf.dtype)

    def accumulate(j, buf, carry):
        acc_ref[...] += _dot(ckvt_ref[0, j], buf[...])
        return carry

    pipeline(n_tiles, stage_probs, accumulate, 0, (pa_ref, pb_ref))

    l = acc_ref[KV_RANK:KV_RANK + 1, :]
    outs = []
    for hd in range(H_A):
        o_t = acc_ref[0:KV_RANK, head_cols[hd]] / l[:, head_cols[hd]]
        outs.append(_dot(o_t.T.astype(MXU_DTYPE), wuv_ref[hd]))
    out_ref[0] = jnp.concatenate(outs, axis=-1).astype(out_ref.dtype)


def _dsa_call(qidx, widxt, qlat, kidx, ckv, ckvt, biasn, wuv, bsz, s_len):
    topk = min(TOPK_MAX, s_len // 4)
    nblk = s_len // QBLK
    qblock_t = lambda depth: pl.BlockSpec((1, 1, depth, H_A * QBLK), lambda b, i: (b, i, 0, 0))
    const = lambda shape: pl.BlockSpec(shape, lambda b, i: (0,) * len(shape))
    return pl.pallas_call(
        functools.partial(_dsa_kernel, s_len=s_len, topk=topk),
        grid=(bsz, nblk),
        in_specs=[
            qblock_t(IDX_DIM),
            pl.BlockSpec((1, SUBLANES, QBLK), lambda b, i: (b, 0, i)),
            qblock_t(KV_RANK),
            pl.BlockSpec((1, s_len, IDX_DIM), lambda b, i: (b, 0, 0)),
            pl.BlockSpec((1, s_len, KV_RANK), lambda b, i: (b, 0, 0)),
            pl.BlockSpec((1, nblk, KV_EXT, QBLK), lambda b, i: (b, 0, 0, 0)),
            const((3, QBLK, H_A * QBLK)),
            const((H_A, KV_RANK, HEAD_DIM)),
        ],
        out_specs=pl.BlockSpec((1, QBLK, H_A * HEAD_DIM), lambda b, i: (b, i, 0)),
        out_shape=jax.ShapeDtypeStruct((bsz, s_len, H_A * HEAD_DIM), MXU_DTYPE),
        scratch_shapes=[
            pltpu.VMEM((s_len, QBLK), F32),
            pltpu.VMEM((s_len, QBLK), HALF_DTYPE),
            pltpu.VMEM((s_len, QBLK), HALF_DTYPE),
            pltpu.VMEM((s_len, H_A * QBLK), F32),
            pltpu.VMEM((KV_EXT, H_A * QBLK), F32),
            pltpu.VMEM((QBLK, H_A * QBLK), F32),
            pltpu.VMEM((QBLK, H_A * QBLK), F32),
            pltpu.VMEM((QBLK, H_A * QBLK), MXU_DTYPE),
            pltpu.VMEM((QBLK, H_A * QBLK), MXU_DTYPE),
            pltpu.VMEM((s_len, QBLK), HALF_DTYPE),
        ],
        compiler_params=pltpu.CompilerParams(
            dimension_semantics=("parallel", "arbitrary"), vmem_limit_bytes=VMEM_LIMIT),
        name="dsa_mixer",
    )(qidx, widxt, qlat, kidx, ckv, ckvt, biasn, wuv)


def _split3(x):
    a = x.astype(MXU_DTYPE)
    r = x - a.astype(F32)
    b = r.astype(MXU_DTYPE)
    c = (r - b.astype(F32)).astype(MXU_DTYPE)
    return a, b, c


def _gla_kernel(qb_ref, kb_ref, vb_ref, glr_ref, rb_ref, wg_ref, bg_ref, gn_ref,
                out_ref, st_ref):
    n_chunks = GLA_SUB // CHUNK
    assert n_chunks * DK_B == GLA_SUB

    @pl.when(pl.program_id(1) == 0)
    def _():
        st_ref[...] = jnp.zeros(st_ref.shape, F32)

    r = lax.broadcasted_iota(jnp.int32, (GLA_SUB, GLA_SUB), 0)
    c = lax.broadcasted_iota(jnp.int32, (GLA_SUB, GLA_SUB), 1)
    same_chunk = (r // CHUNK) == (c // CHUNK)
    causal = same_chunk & (c <= r)
    in_block = (r // CHUNK) == (c // DK_B)
    tril = causal.astype(MXU_DTYPE)
    ones = same_chunk.astype(MXU_DTYPE)
    w_hi, w_lo, _ = _split3(wg_ref[...])

    def expand(x):
        return jnp.where(in_block, jnp.concatenate([x] * n_chunks, axis=1), 0.0).astype(MXU_DTYPE)

    def decays(rows):
        g_hi, g_lo, _ = _split3(glr_ref[0, rows, :])
        z = _dot(g_hi, w_hi) + (_dot(g_hi, w_lo) + _dot(g_lo, w_hi)) + bg_ref[...]
        g = (jnp.minimum(z, 0.0) - jnp.log1p(jnp.exp(-jnp.abs(z)))) / GATE_TEMP
        g1, g2, g3 = _split3(g)
        b = _dot(tril, g1) + (_dot(tril, g2) + _dot(tril, g3))
        b_last = _dot(ones, g1) + (_dot(ones, g2) + _dot(ones, g3))
        q = qb_ref[0, rows, :].astype(F32) * (DK_B ** -0.5)
        k = kb_ref[0, rows, :].astype(F32)
        qe = q * jnp.exp(b)
        ke = (k * jnp.exp(-b)).astype(MXU_DTYPE)
        kd = k * jnp.exp(b_last - b)
        return qe, ke, kd, jnp.exp(b_last)

    prepared = [decays(slice(t * GLA_SUB, (t + 1) * GLA_SUB)) for t in range(TS_GLA // GLA_SUB)]
    state = [st_ref[hd] for hd in range(H_B)]
    for t, (qe, ke, kd, decay) in enumerate(prepared):
        rows = slice(t * GLA_SUB, (t + 1) * GLA_SUB)
        for hd in range(H_B):
            ks = slice(hd * DK_B, (hd + 1) * DK_B)
            vs = slice(hd * DV_B, (hd + 1) * DV_B)
            v = vb_ref[0, rows, vs]
            qe_h = qe[:, ks]
            att = jnp.where(causal, _dot_nt(qe_h.astype(MXU_DTYPE), ke[:, ks]), 0.0)
            kv_t = _dot_tn(v, expand(kd[:, ks]))
            st = state[hd]
            starts = []
            for ci in range(n_chunks):
                starts.append(st)
                st = (st * decay[ci * CHUNK:ci * CHUNK + 1, ks]
                      + kv_t[:, ci * DK_B:(ci + 1) * DK_B])
            state[hd] = st
            st_all = jnp.concatenate(starts, axis=1).astype(MXU_DTYPE)
            o = _dot(att.astype(MXU_DTYPE), v) + _dot_nt(expand(qe_h), st_all)
            o = o * lax.rsqrt(jnp.mean(o * o, axis=-1, keepdims=True) + EPS) * gn_ref[:, vs]
            rg = rb_ref[0, rows, vs].astype(F32)
            o = o * (rg / (1.0 + jnp.exp(-rg)))
            out_ref[0, rows, vs] = o.astype(out_ref.dtype)
    for hd in range(H_B):
        st_ref[hd] = state[hd]


def _gla_call(qb, kb, vb, glr, rb, wg, bg, gn, bsz, s_len):
    ts = TS_GLA
    tok = lambda width: pl.BlockSpec((1, ts, width), lambda b, i: (b, i, 0))
    const = lambda shape: pl.BlockSpec(shape, lambda b, i: (0,) * len(shape))
    return pl.pallas_call(
        _gla_kernel,
        grid=(bsz, s_len // ts),
        in_specs=[tok(H_B * DK_B), tok(H_B * DK_B), tok(H_B * DV_B), tok(LANES), tok(H_B * DV_B),
                  const((LANES, H_B * DK_B)), const((1, H_B * DK_B)), const((1, H_B * DV_B))],
        out_specs=tok(H_B * DV_B),
        out_shape=jax.ShapeDtypeStruct((bsz, s_len, H_B * DV_B), MXU_DTYPE),
        scratch_shapes=[pltpu.VMEM((H_B, DV_B, DK_B), F32)],
        compiler_params=pltpu.CompilerParams(
            dimension_semantics=("parallel", "arbitrary"), vmem_limit_bytes=VMEM_LIMIT),
        name="gla_mixer",
    )(qb, kb, vb, glr, rb, wg, bg, gn)


SWA_SUB = WINDOW
SWA_WIN = SWA_SUB + WINDOW


def _swa_window_start(tile_start, sub):
    return jnp.maximum(tile_start + sub * SWA_SUB - WINDOW, 0)


def _swa_kernel(sink_ref, qc_ref, kc_ref, vc_ref, mb_ref, out_ref):
    i = pl.program_id(1)
    grp = H_C // KV_C
    for sub in range(TQ_SWA // SWA_SUB):
        qrows = slice(sub * SWA_SUB, (sub + 1) * SWA_SUB)
        start = pl.multiple_of(_swa_window_start(i * TQ_SWA, sub), WINDOW)
        k = kc_ref[0, pl.ds(start, SWA_WIN), :]
        v = vc_ref[0, pl.ds(start, SWA_WIN), :]
        outs = []
        for hd in range(H_C):
            kv = hd // grp
            q = qc_ref[0, qrows, hd * HEAD_DIM:(hd + 1) * HEAD_DIM]
            lg = (_dot_nt(q, k[:, kv * HEAD_DIM:(kv + 1) * HEAD_DIM]) * (HEAD_DIM ** -0.5)
                  + mb_ref[0, hd, sub])
            sink = sink_ref[hd]
            m = jnp.maximum(jnp.max(lg, axis=-1, keepdims=True), sink)
            e = jnp.exp(lg - m)
            p = e / (jnp.sum(e, axis=-1, keepdims=True) + jnp.exp(sink - m))
            outs.append(_dot(p.astype(MXU_DTYPE), v[:, kv * HEAD_DIM:(kv + 1) * HEAD_DIM]))
        out_ref[0, qrows, :] = jnp.concatenate(outs, axis=-1).astype(out_ref.dtype)


def _swa_call(sinks, qc, kc, vc, maskbias, bsz, s_len):
    tq = TQ_SWA
    return pl.pallas_call(
        _swa_kernel,
        grid=(bsz, s_len // tq),
        in_specs=[
            pl.BlockSpec(memory_space=pltpu.SMEM),
            pl.BlockSpec((1, tq, H_C * HEAD_DIM), lambda b, i: (b, i, 0)),
            pl.BlockSpec((1, s_len, KV_C * HEAD_DIM), lambda b, i: (b, 0, 0)),
            pl.BlockSpec((1, s_len, KV_C * HEAD_DIM), lambda b, i: (b, 0, 0)),
            pl.BlockSpec((1, H_C, tq // SWA_SUB, SWA_SUB, SWA_WIN),
                         lambda b, i: (jnp.minimum(i, 1), 0, 0, 0, 0)),
        ],
        out_specs=pl.BlockSpec((1, tq, H_C * HEAD_DIM), lambda b, i: (b, i, 0)),
        out_shape=jax.ShapeDtypeStruct((bsz, s_len, H_C * HEAD_DIM), MXU_DTYPE),
        compiler_params=pltpu.CompilerParams(
            dimension_semantics=("parallel", "arbitrary"), vmem_limit_bytes=VMEM_LIMIT),
        name="swa_mixer",
    )(sinks, qc, kc, vc, maskbias)


def _ffn_kernel(x_ref, oa_ref, ob_ref, oc_ref, gf_ref, gfin_ref,
                wo_hbm, wg_hbm, wu_hbm, wd_hbm, out_ref,
                wo_ref, wg_ref, wu_ref, wd_ref, sems, *, final_norm):
    @pl.when(pl.program_id(0) == 0)
    def _():
        pairs = ((wo_hbm, wo_ref), (wg_hbm, wg_ref), (wu_hbm, wu_ref), (wd_hbm, wd_ref))
        copies = [pltpu.make_async_copy(src, dst, sems.at[k]) for k, (src, dst) in enumerate(pairs)]
        for cp in copies:
            cp.start()
        for cp in copies:
            cp.wait()

    na = H_A * HEAD_DIM
    nb = H_B * DV_B
    x1 = (x_ref[...] + _dot(oa_ref[...], wo_ref[0:na, :])
          + _dot(ob_ref[...], wo_ref[na:na + nb, :])
          + _dot(oc_ref[...], wo_ref[na + nb:, :]))
    h = _rms(x1, gf_ref[...]).astype(MXU_DTYPE)
    y = x1
    for j in range(D_FF // TF_FFN):
        cols = slice(j * TF_FFN, (j + 1) * TF_FFN)
        gate = _dot(h, wg_ref[:, cols])
        up = _dot(h, wu_ref[:, cols])
        a = (gate / (1.0 + jnp.exp(-gate))) * up
        y = y + _dot(a.astype(MXU_DTYPE), wd_ref[cols, :])
    if final_norm:
        y = _rms(y, gfin_ref[...])
    out_ref[...] = y


def _ffn_call(x2, oa, ob, oc, wo, gf, wg, wu, wd, gfin, final_norm):
    t_tok = x2.shape[0]
    tm = TM_FFN
    tok = lambda width: pl.BlockSpec((tm, width), lambda i: (i, 0))
    const = lambda shape: pl.BlockSpec(shape, lambda i: (0,) * len(shape))
    in_hbm = pl.BlockSpec(memory_space=pl.ANY)
    return pl.pallas_call(
        functools.partial(_ffn_kernel, final_norm=final_norm),
        grid=(t_tok // tm,),
        in_specs=[tok(D_MODEL), tok(H_A * HEAD_DIM), tok(H_B * DV_B), tok(H_C * HEAD_DIM),
                  const((1, D_MODEL)), const((1, D_MODEL)), in_hbm, in_hbm, in_hbm, in_hbm],
        out_specs=tok(D_MODEL),
        out_shape=jax.ShapeDtypeStruct((t_tok, D_MODEL), F32),
        scratch_shapes=[pltpu.VMEM((D_MODEL, D_MODEL), MXU_DTYPE),
                        pltpu.VMEM((D_MODEL, D_FF), MXU_DTYPE),
                        pltpu.VMEM((D_MODEL, D_FF), MXU_DTYPE),
                        pltpu.VMEM((D_FF, D_MODEL), MXU_DTYPE),
                        pltpu.SemaphoreType.DMA((4,))],
        compiler_params=pltpu.CompilerParams(
            dimension_semantics=("arbitrary",), vmem_limit_bytes=VMEM_LIMIT),
        name="out_proj_ffn",
    )(x2, oa, ob, oc, gf, gfin, wo, wg, wu, wd)


def _rel_bucket(rel):
    nb = NUM_BUCKETS // 2
    max_exact = nb // 2
    n = jnp.abs(rel)
    side = jnp.where(rel > 0, nb, 0)
    nf = jnp.maximum(n, 1).astype(jnp.float32)
    large = max_exact + (jnp.log(nf / max_exact) / math.log(MAX_DISTANCE / max_exact)
                         * (nb - max_exact)).astype(jnp.int32)
    large = jnp.minimum(large, nb - 1)
    return side + jnp.where(n < max_exact, n, large)


def _bucket_ids(ndim, axis):
    shape = [1] * ndim
    shape[axis] = NUM_BUCKETS
    return jnp.arange(NUM_BUCKETS, dtype=jnp.int32).reshape(shape)


def _dsa_bias_tables(rel_bias):
    table = rel_bias[:, :H_A].astype(F32)
    far = _rel_bucket(jnp.full((), -2 * QBLK, jnp.int32))
    table = table - jnp.sum(jnp.where(_bucket_ids(2, 0) == far, table, 0.0), axis=0, keepdims=True)
    kk = jnp.arange(QBLK, dtype=jnp.int32)[:, None]
    qq = jnp.arange(QBLK, dtype=jnp.int32)[None, :]
    near = []
    for shift in (0, QBLK):
        bucket = _rel_bucket(kk - shift - qq)[:, None, None, :]
        b = jnp.sum(jnp.where(bucket == _bucket_ids(4, 1), table[None, :, :, None], 0.0), axis=1)
        near.append(b.reshape(QBLK, H_A * QBLK))
    near.append(jnp.zeros((QBLK, H_A * QBLK), F32))
    return jnp.stack(near)


def _swa_maskbias(rel_bias):
    table = rel_bias[:, H_A:].astype(F32)
    out = []
    for tile_start in (0, TQ_SWA):
        subs = []
        for sub in range(TQ_SWA // SWA_SUB):
            qq = tile_start + sub * SWA_SUB + jnp.arange(SWA_SUB, dtype=jnp.int32)[:, None]
            kk = (max(tile_start + sub * SWA_SUB - WINDOW, 0)
                  + jnp.arange(SWA_WIN, dtype=jnp.int32)[None, :])
            dc = qq // CHUNK - kk // CHUNK
            band = (dc >= 0) & (dc <= WIN_CHUNKS)
            bucket = _rel_bucket(kk - qq)[None, None]
            b = jnp.sum(jnp.where(bucket == _bucket_ids(4, 0), table[:, :, None, None], 0.0), axis=0)
            subs.append(jnp.where(band[None], b, NEG_INF))
        out.append(jnp.stack(subs, axis=1))
    return jnp.stack(out)


def kernel(x, w_in, w_out, norm_mix, norm_ffn, kv_norm, w_uk, w_uv, w_gate2, b_gate,
           gla_norm, sinks, rel_bias, w_ffn_gate, w_ffn_up, w_ffn_down, final_norm):
    bsz, s_len, _ = x.shape
    t_tok = bsz * s_len
    wd = MXU_DTYPE
    biasn = _dsa_bias_tables(rel_bias)
    maskbias = _swa_maskbias(rel_bias)
    x2 = x.reshape(t_tok, D_MODEL)
    row = lambda v: v.reshape(1, -1).astype(F32)
    for l in range(DEPTH):
        w_windows = _split_w_in(w_in[l].astype(wd))
        wuk = jnp.transpose(w_uk[l], (1, 2, 0)).astype(wd)
        wuv = jnp.transpose(w_uv[l], (1, 0, 2)).astype(wd)
        (qlat, ckv, ckvt, qidx, kidx, widxt, qb, kb, vb, glr, rb, qc, kc, vc) = _proj_call(
            x2, row(norm_mix[l]), w_windows, wuk, row(kv_norm[l]), bsz, s_len)
        b3 = lambda a: a.reshape(bsz, s_len, a.shape[-1])
        out_a = _dsa_call(qidx, widxt, qlat, b3(kidx), b3(ckv), ckvt,
                          biasn, wuv, bsz, s_len)
        wg2 = jnp.zeros((LANES, H_B * DK_B), F32).at[:GATE_RANK].set(w_gate2[l])
        out_b = _gla_call(b3(qb), b3(kb), b3(vb), b3(glr), b3(rb), wg2,
                          row(b_gate[l]), row(gla_norm[l]), bsz, s_len)
        out_c = _swa_call(sinks[l].astype(F32), b3(qc), b3(kc), b3(vc), maskbias, bsz, s_len)
        f2 = lambda a: a.reshape(t_tok, a.shape[-1])
        x2 = _ffn_call(x2, f2(out_a), f2(out_b), f2(out_c), w_out[l].astype(wd),
                       row(norm_ffn[l]), w_ffn_gate[l].astype(wd), w_ffn_up[l].astype(wd),
                       w_ffn_down[l].astype(wd), row(final_norm), l == DEPTH - 1)
    return x2.reshape(bsz, s_len, D_MODEL)
```
